```python
import math
import jax, jax.numpy as jnp
from jax import lax
import numpy as np

D_MODEL = 2048
BATCH = 1
SEQ = 8192
DEPTH = 4

BRANCH_WIDTH = 1024
N_BRANCH = 3
DIL_PAIRS = ((128, 1), (512, 4), (2048, 16))
N_DIL_GROUPS = 3
A_HEADS = 8
A_HEAD_DIM = 128
HY_BANDS = 16
HY_EMB_DIM = 1 + 2 * HY_BANDS
HY_FFN = 64
HY_ORDER = 2
HY_SHORT = 3
HY_TARGET = 1e-2
HY_FAST_PCT = 0.3
HY_SLOW_PCT = 1.5
C_HEADS = 8
C_QK_DIM = 64
C_V_DIM = 128
Q_BLOCK = 128
N_BUCKETS = 32
REL_MAX_DIST = 1024
N_BIAS_HEADS = N_DIL_GROUPS * A_HEADS + C_HEADS
NORM_EPS = 1e-6
NEG = -1e30

A_QKV_COLS = N_DIL_GROUPS * 3 * A_HEADS * A_HEAD_DIM
B_IN_COLS = (HY_ORDER + 1) * BRANCH_WIDTH
C_QKV_COLS = 2 * 2 * C_HEADS * C_QK_DIM + C_HEADS * C_V_DIM
MERGE_COLS = N_BRANCH * D_MODEL
COL_SIZES = (A_QKV_COLS, BRANCH_WIDTH, B_IN_COLS, BRANCH_WIDTH, C_QKV_COLS, BRANCH_WIDTH, MERGE_COLS)
IN_COLS = int(sum(COL_SIZES))
SPLITS = [int(s) for s in np.cumsum(COL_SIZES)[:-1]]

kernel_name = "hybrid_dilated_hyena_diffattn_encoder"


def rms_norm(x, g):
    xf = x.astype(jnp.float32)
    y = xf * lax.rsqrt(jnp.mean(xf * xf, axis=-1, keepdims=True) + NORM_EPS)
    return (y * g.astype(jnp.float32)).astype(x.dtype)


def t5_bucket(rel):
    nb = N_BUCKETS // 2
    max_exact = nb // 2
    ret = (rel > 0).astype(jnp.int32) * nb
    n = jnp.abs(rel)
    nf = jnp.maximum(n, 1).astype(jnp.float32)
    large = max_exact + (jnp.log(nf / max_exact) / math.log(REL_MAX_DIST / max_exact) * (nb - max_exact)).astype(jnp.int32)
    large = jnp.minimum(large, nb - 1)
    return ret + jnp.where(n < max_exact, n, large)


def dilated_group(q, k, v, bias_tab, dil, reach):
    B, S, H, hd = q.shape
    n = S // dil
    blk = reach
    nb = -(-n // blk)
    n_pad = nb * blk

    def split_res(t):
        return t.reshape(B, n, dil, H, hd).transpose(0, 2, 3, 1, 4)

    qs, ks, vs = split_res(q), split_res(k), split_res(v)
    qb = jnp.pad(qs, ((0, 0), (0, 0), (0, 0), (0, n_pad - n), (0, 0))).reshape(B, dil, H, nb, blk, hd)

    def band(t):
        tp = jnp.pad(t, ((0, 0), (0, 0), (0, 0), (blk, n_pad - n + blk), (0, 0)))
        tb = tp.reshape(B, dil, H, nb + 2, blk, hd)
        return jnp.concatenate([tb[:, :, :, :-2], tb[:, :, :, 1:-1], tb[:, :, :, 2:]], axis=4)

    kb, vb = band(ks), band(vs)
    t_idx = jnp.arange(blk)[:, None]
    u_idx = jnp.arange(3 * blk)[None, :]
    rel = u_idx - blk - t_idx
    m_k = jnp.arange(nb)[:, None, None] * blk - blk + u_idx
    valid = (jnp.abs(rel) <= reach)[None] & (m_k >= 0) & (m_k < n)
    bias = bias_tab[t5_bucket(rel * dil)].astype(jnp.float32).transpose(2, 0, 1)
    scale = 1.0 / math.sqrt(hd)
    logits = jnp.einsum('bdhnqc,bdhnkc->bdhnqk', qb, kb, preferred_element_type=jnp.float32) * scale
    logits = jnp.where(valid[None, None, None], logits + bias[None, None, :, None], NEG)
    lse = jax.nn.logsumexp(logits, axis=-1)
    p = jnp.exp(logits - lse[..., None])
    o = jnp.einsum('bdhnqk,bdhnkc->bdhnqc', p.astype(v.dtype), vb)
    o = o.reshape(B, dil, H, n_pad, hd)[:, :, :, :n].transpose(0, 3, 1, 2, 4).reshape(B, S, H, hd)
    lse = lse.reshape(B, dil, H, n_pad)[..., :n].transpose(0, 3, 1, 2).reshape(B, S, H)
    return o, lse


def diff_attention(q, k, v, lam, bias_tab):
    B, S, H, _, dq = q.shape
    nqb = S // Q_BLOCK
    qb = q.reshape(B, nqb, Q_BLOCK, H, 2, dq).transpose(1, 0, 2, 3, 4, 5)
    kpos = jnp.arange(S)
    scale = 1.0 / math.sqrt(dq)

    def one_block(args):
        qblk, bi = args
        qpos = bi * Q_BLOCK + jnp.arange(Q_BLOCK)
        bias = bias_tab[t5_bucket(kpos[None, :] - qpos[:, None])].astype(jnp.float32).transpose(2, 0, 1)
        logits = jnp.einsum('bqhcd,bkhcd->bhcqk', qblk, k, preferred_element_type=jnp.float32) * scale
        p = jax.nn.softmax(logits + bias[None, :, None], axis=-1)
        a = p[:, :, 0] - lam * p[:, :, 1]
        return jnp.einsum('bhqk,bkhd->bqhd', a.astype(v.dtype), v)

    o = lax.map(one_block, (qb, jnp.arange(nqb)))
    return o.transpose(1, 0, 2, 3, 4).reshape(B, S, H, v.shape[-1])


def short_conv(u, w):
    C = u.shape[-1]
    pad = HY_SHORT // 2
    return lax.conv_general_dilated(u, w[:, None, :].astype(u.dtype), window_strides=(1,), padding=((pad, pad),),
                                    dimension_numbers=('NWC', 'WIO', 'NWC'), feature_group_count=C)


def hyena_filters(L, w1, b1, freq, w2, b2, w3):
    f32 = jnp.float32
    t = jnp.linspace(0.0, 1.0, L, dtype=f32)[:, None]
    w = 2.0 * math.pi * jnp.arange(L, dtype=f32)[:, None] / L
    fb = jnp.linspace(1e-4, HY_BANDS - 1, HY_BANDS, dtype=f32)[None]
    z = jnp.concatenate([t, jnp.cos(fb * w), -jnp.sin(fb * w)], axis=-1)
    freq = freq.astype(f32)
    hid = jnp.sin(freq[0] * (z @ w1.astype(f32) + b1.astype(f32)))
    hid = jnp.sin(freq[1] * (hid @ w2.astype(f32) + b2.astype(f32)))
    h = (hid @ w3.astype(f32)).reshape(L, HY_ORDER, 2, BRANCH_WIDTH)
    max_decay = math.log(HY_TARGET) / HY_FAST_PCT
    min_decay = math.log(HY_TARGET) / HY_SLOW_PCT
    deltas = jnp.linspace(min_decay, max_decay, BRANCH_WIDTH, dtype=f32)
    decay = jnp.exp(-t * jnp.abs(deltas)[None])
    return h * decay[:, None, None, :]


def fft_conv(z, h):
    L = z.shape[1]
    n = 2 * L
    Z = jnp.fft.rfft(z, n=n, axis=1)
    Hf = jnp.fft.rfft(h, n=n, axis=0)
    return jnp.fft.irfft(Z * Hf[None], n=n, axis=1)[:, :L]


def bidir_long_conv(z, h_fwd, h_bwd, skip):
    zf = z.astype(jnp.float32)
    y = fft_conv(zf, h_fwd) + jnp.flip(fft_conv(jnp.flip(zf, 1), h_bwd), 1) + skip.astype(jnp.float32) * zf
    return y.astype(z.dtype)


def setup_inputs(seed: int = 0) -> dict:
    key = jax.random.key(seed)
    ks = jax.random.split(key, 20)
    f32 = jnp.float32
    nrm = lambda k, shape, s: jax.random.normal(k, shape, f32) * s
    return {
        "x": nrm(ks[0], (BATCH, SEQ, D_MODEL), 1.0),
        "norm_g": 1.0 + nrm(ks[1], (DEPTH, D_MODEL), 0.02),
        "final_g": 1.0 + nrm(ks[2], (D_MODEL,), 0.02),
        "w_in": nrm(ks[3], (DEPTH, D_MODEL, IN_COLS), D_MODEL ** -0.5),
        "merge_b": nrm(ks[4], (DEPTH, N_BRANCH, D_MODEL), 0.02),
        "rel_bias": nrm(ks[5], (N_BUCKETS, N_BIAS_HEADS), 0.2),
        "hy_conv": nrm(ks[6], (DEPTH, HY_SHORT, B_IN_COLS), HY_SHORT ** -0.5),
        "hy_w1": nrm(ks[7], (DEPTH, HY_EMB_DIM, HY_FFN), HY_EMB_DIM ** -0.5),
        "hy_b1": nrm(ks[8], (DEPTH, HY_FFN), 0.1),
        "hy_freq": 1.0 + nrm(ks[9], (DEPTH, 2, HY_FFN), 0.05),
        "hy_w2": nrm(ks[10], (DEPTH, HY_FFN, HY_FFN), HY_FFN ** -0.5),
        "hy_b2": nrm(ks[11], (DEPTH, HY_FFN), 0.1),
        "hy_w3": nrm(ks[12], (DEPTH, HY_FFN, HY_ORDER * 2 * BRANCH_WIDTH), 0.05 * HY_FFN ** -0.5),
        "hy_skip": nrm(ks[13], (DEPTH, HY_ORDER, BRANCH_WIDTH), 0.5),
        "diff_lam": nrm(ks[14], (DEPTH, 4, C_QK_DIM), 0.1),
        "diff_g": 1.0 + nrm(ks[15], (DEPTH, C_V_DIM), 0.02),
        "w_proj": nrm(ks[16], (DEPTH, N_BRANCH, BRANCH_WIDTH, D_MODEL), BRANCH_WIDTH ** -0.5),
        "w_out": nrm(ks[17], (DEPTH, D_MODEL, D_MODEL), D_MODEL ** -0.5),
    }


def reference(x, norm_g, final_g, w_in, merge_b, rel_bias, hy_conv, hy_w1, hy_b1, hy_freq, hy_w2, hy_b2, hy_w3,
              hy_skip, diff_lam, diff_g, w_proj, w_out):
    B, S, _ = x.shape
    W = BRANCH_WIDTH
    for l in range(DEPTH):
        h = rms_norm(x, norm_g[l])
        proj = h @ w_in[l]
        a_qkv, a_gate, b_in, b_gate, c_qkv, c_gate, merge = jnp.split(proj, SPLITS, axis=-1)

        a_qkv = a_qkv.reshape(B, S, N_DIL_GROUPS, 3, A_HEADS, A_HEAD_DIM)
        outs, lses = [], []
        for g, (win, dil) in enumerate(DIL_PAIRS):
            o_g, lse_g = dilated_group(a_qkv[:, :, g, 0], a_qkv[:, :, g, 1], a_qkv[:, :, g, 2],
                                       rel_bias[:, g * A_HEADS:(g + 1) * A_HEADS], dil, win // (2 * dil))
            outs.append(o_g)
            lses.append(lse_g)
        wts = jax.nn.softmax(jnp.stack(lses), axis=0)
        a_out = jnp.einsum('gbsh,gbshd->bshd', wts, jnp.stack(outs).astype(jnp.float32)).astype(x.dtype)
        a_out = a_out.reshape(B, S, W) * jax.nn.silu(a_gate)

        u = short_conv(b_in, hy_conv[l])
        v, x1, x2 = jnp.split(u, 3, axis=-1)
        filt = hyena_filters(S, hy_w1[l], hy_b1[l], hy_freq[l], hy_w2[l], hy_b2[l], hy_w3[l])
        z = v
        for o, gate in enumerate((x1, x2)):
            z = gate * bidir_long_conv(z, filt[:, o, 0], filt[:, o, 1], hy_skip[l, o])
        b_out = z * jax.nn.silu(b_gate)

        qk_w = 2 * C_HEADS * C_QK_DIM
        cq = c_qkv[..., :qk_w].reshape(B, S, C_HEADS, 2, C_QK_DIM)
        ck = c_qkv[..., qk_w:2 * qk_w].reshape(B, S, C_HEADS, 2, C_QK_DIM)
        cv = c_qkv[..., 2 * qk_w:].reshape(B, S, C_HEADS, C_V_DIM)
        lam_init = 0.8 - 0.6 * math.exp(-0.3 * l)
        dl = diff_lam[l].astype(jnp.float32)
        lam = jnp.exp(jnp.sum(dl[0] * dl[1])) - jnp.exp(jnp.sum(dl[2] * dl[3])) + lam_init
        c_o = diff_attention(cq, ck, cv, lam, rel_bias[:, N_DIL_GROUPS * A_HEADS:])
        c_o = rms_norm(c_o, diff_g[l]) * (1.0 - lam_init)
        c_out = c_o.reshape(B, S, W) * jax.nn.silu(c_gate)

        branches = jnp.stack([a_out, b_out, c_out], axis=2)
        proj_b = jnp.einsum('bsnw,nwd->bsnd', branches, w_proj[l])
        gates = jax.nn.sigmoid(merge.reshape(B, S, N_BRANCH, D_MODEL) + merge_b[l])
        y = jnp.sum(gates * proj_b, axis=2)
        x = x + y @ w_out[l]
    return rms_norm(x, final_g)
```

```python
import functools
import math

import numpy as np
import jax
import jax.numpy as jnp
from jax import lax
from jax.experimental import pallas as pl
from jax.experimental.pallas import tpu as pltpu

F32 = jnp.float32
BF16 = jnp.bfloat16

LANES = 128
D_MODEL = 2048
BRANCH_W = 1024
N_BRANCH = 3
DIL_PAIRS = ((128, 1), (512, 4), (2048, 16))
A_HEADS = 8
A_HEAD_DIM = 128
REACH = 64
C_HEADS = 8
C_QK_DIM = 64
C_V_DIM = 128
HY_BANDS = 16
HY_FFN = 64
HY_ORDER = 2
HY_TARGET = 1e-2
HY_FAST_PCT = 0.3
HY_SLOW_PCT = 1.5
N_BUCKETS = 32
REL_MAX_DIST = 1024
NORM_EPS = 1e-6
NEG = -1e30

A_QKV_COLS = 3 * 3 * A_HEADS * A_HEAD_DIM
P1_COLS = BRANCH_W + 3 * BRANCH_W + BRANCH_W
C_QKV_COLS = 3 * C_HEADS * C_V_DIM
P2_COLS = BRANCH_W + N_BRANCH * D_MODEL
COL_A, COL_P1, COL_C, COL_P2 = 0, A_QKV_COLS, A_QKV_COLS + P1_COLS, A_QKV_COLS + P1_COLS + C_QKV_COLS

VMEM_LIMIT = 56 * 1024 * 1024
HIGHEST = lax.Precision.HIGHEST


def _bucket_thresholds():
    nb = N_BUCKETS // 2
    max_exact = nb // 2
    thr = []
    for k in range(1, nb - max_exact):
        n = max_exact
        while int(math.log(n / max_exact) / math.log(REL_MAX_DIST / max_exact) * (nb - max_exact)) < k:
            n += 1
        thr.append(n)
    return tuple(thr)


BUCKET_THR = _bucket_thresholds()
FAR_DIST = BUCKET_THR[-1]


def _params(*sem):
    return pltpu.CompilerParams(dimension_semantics=sem, vmem_limit_bytes=VMEM_LIMIT)


def _rel_bias(rel, tab_ref, col):
    nb = N_BUCKETS // 2
    n = jnp.abs(rel)
    large = jnp.full(rel.shape, nb // 2, jnp.int32)
    for thr in BUCKET_THR:
        large = large + (n >= thr).astype(jnp.int32)
    bucket = jnp.where(n < nb // 2, n, large) + jnp.where(rel > 0, nb, 0)
    out = jnp.full(rel.shape, tab_ref[0, col], F32)
    for j in range(1, N_BUCKETS):
        out = jnp.where(bucket == j, tab_ref[j, col], out)
    return out


def _silu(x):
    return x * (1.0 / (1.0 + jnp.exp(-x)))


def _rmsnorm_kernel(x_ref, g_ref, o_ref):
    x = x_ref[...]
    ms = jnp.mean(x * x, axis=-1, keepdims=True)
    o_ref[...] = (x * lax.rsqrt(ms + NORM_EPS) * g_ref[...]).astype(o_ref.dtype)


def _rmsnorm(x, g, out_dtype, tm=512):
    S, D = x.shape
    return pl.pallas_call(
        _rmsnorm_kernel,
        grid=(S // tm,),
        in_specs=[pl.BlockSpec((tm, D), lambda i: (i, 0)), pl.BlockSpec((1, D), lambda i: (0, 0))],
        out_specs=pl.BlockSpec((tm, D), lambda i: (i, 0)),
        out_shape=jax.ShapeDtypeStruct((S, D), out_dtype),
        compiler_params=_params("parallel"),
        name="rmsnorm",
    )(x, g.reshape(1, D))


def _proj_kernel(h_ref, w_ref, o_ref, wb_ref, *, heads):
    @pl.when(pl.program_id(1) == 0)
    def _():
        wb_ref[...] = w_ref[...].astype(BF16)

    acc = jnp.dot(h_ref[...], wb_ref[...], preferred_element_type=F32)
    if heads:
        for c in range(acc.shape[1] // LANES):
            o_ref[c] = acc[:, c * LANES:(c + 1) * LANES].astype(o_ref.dtype)
    else:
        o_ref[...] = acc.astype(o_ref.dtype)


def _in_proj(h, w_in, layer, col0, ncols, heads, tm=1024, tn=1024):
    S, D = h.shape
    assert col0 % tn == 0 and ncols % tn == 0
    jb = col0 // tn
    if heads:
        out_shape = jax.ShapeDtypeStruct((ncols // LANES, S, LANES), BF16)
        out_spec = pl.BlockSpec((tn // LANES, tm, LANES), lambda j, i: (j, i, 0))
    else:
        out_shape = jax.ShapeDtypeStruct((S, ncols), F32)
        out_spec = pl.BlockSpec((tm, tn), lambda j, i: (i, j))
    return pl.pallas_call(
        functools.partial(_proj_kernel, heads=heads),
        grid=(ncols // tn, S // tm),
        in_specs=[pl.BlockSpec((tm, D), lambda j, i: (i, 0)),
                  pl.BlockSpec((None, D, tn), lambda j, i: (layer, 0, jb + j))],
        out_specs=out_spec,
        out_shape=out_shape,
        scratch_shapes=[pltpu.VMEM((D, tn), BF16)],
        compiler_params=_params("parallel", "arbitrary"),
        name="in_proj",
    )(h, w_in)


def _dil_kernel(*refs, dil, col0, first, last, n, tq):
    it = iter(refs)
    tab_ref, q_ref, k_ref, v_ref = next(it), next(it), next(it), next(it)
    po_ref = pl_ref = gate_ref = lse_ref = None
    if not first:
        po_ref, pl_ref = next(it), next(it)
    if last:
        gate_ref = next(it)
    o_ref = next(it)
    if not last:
        lse_ref = next(it)
    bm_ref = next(it)

    h, r, i = pl.program_id(0), pl.program_id(1), pl.program_id(2)
    nblk = n // tq
    kw = tq + 2 * REACH
    deltas = (-REACH, 0, -2 * REACH)

    @pl.when((r == 0) & (i == 0))
    def _():
        t = lax.broadcasted_iota(jnp.int32, (tq, kw), 0)
        u = lax.broadcasted_iota(jnp.int32, (tq, kw), 1)
        for idx, delta in enumerate(deltas):
            rel = u + delta - t
            bias = _rel_bias(rel * dil, tab_ref, col0 + h)
            bm_ref[idx] = jnp.where(jnp.abs(rel) <= REACH, bias, NEG)

    ks = pl.multiple_of(jnp.clip(i * tq - REACH, 0, n - kw), REACH)
    sel = jnp.where(i == 0, 1, jnp.where(i == nblk - 1, 2, 0))
    q = q_ref[...]
    k = k_ref[pl.ds(ks, kw), :]
    v = v_ref[pl.ds(ks, kw), :]
    s = lax.dot_general(q, k, (((1,), (1,)), ((), ())), preferred_element_type=F32)
    s = s * (1.0 / math.sqrt(A_HEAD_DIM)) + bm_ref[sel]
    m = jnp.max(s, axis=-1, keepdims=True)
    p = jnp.exp(s - m)
    l = jnp.sum(p, axis=-1, keepdims=True)
    o = jnp.dot(p.astype(BF16), v, preferred_element_type=F32) / l
    lse = jnp.broadcast_to(m + jnp.log(l), o.shape)
    if not first:
        lse_p = pl_ref[...]
        mx = jnp.maximum(lse_p, lse)
        wp = jnp.exp(lse_p - mx)
        wc = jnp.exp(lse - mx)
        tot = wp + wc
        o = (po_ref[...] * wp + o * wc) / tot
        lse = mx + jnp.log(tot)
    if last:
        o_ref[...] = (o * _silu(gate_ref[...])).astype(o_ref.dtype)
    else:
        o_ref[...] = o
        lse_ref[...] = lse


def _dilated_group(tab, aqkv, g, state, gate_src, tq=128):
    win, dil = DIL_PAIRS[g]
    assert win // (2 * dil) == REACH
    nh_total, S, hd = aqkv.shape
    n = S // dil
    assert n % tq == 0 and n // tq >= 2
    first = state is None
    last = gate_src is not None
    assert not last or dil == 1
    view = aqkv.reshape(nh_total, n, dil * hd)
    hq, hk, hv = (g * 3 + 0) * A_HEADS, (g * 3 + 1) * A_HEADS, (g * 3 + 2) * A_HEADS
    in_specs = [pl.BlockSpec(memory_space=pltpu.SMEM),
                pl.BlockSpec((None, tq, hd), lambda h, r, i: (hq + h, i, r)),
                pl.BlockSpec((None, n, hd), lambda h, r, i: (hk + h, 0, r)),
                pl.BlockSpec((None, n, hd), lambda h, r, i: (hv + h, 0, r))]
    args = [tab, view, view, view]
    st_spec = pl.BlockSpec((None, tq, hd), lambda h, r, i: (h, i, r))
    if not first:
        in_specs += [st_spec, st_spec]
        args += [state[0].reshape(A_HEADS, n, dil * hd), state[1].reshape(A_HEADS, n, dil * hd)]
    if last:
        in_specs.append(pl.BlockSpec((tq, hd), lambda h, r, i: (i, h)))
        args.append(gate_src)
        out_shape = jax.ShapeDtypeStruct((S, A_HEADS * hd), BF16)
        out_specs = pl.BlockSpec((tq, hd), lambda h, r, i: (i, h))
    else:
        out_shape = (jax.ShapeDtypeStruct((A_HEADS, n, dil * hd), F32),) * 2
        out_specs = (st_spec, st_spec)
    res = pl.pallas_call(
        functools.partial(_dil_kernel, dil=dil, col0=g * A_HEADS, first=first, last=last, n=n, tq=tq),
        grid=(A_HEADS, dil, n // tq),
        in_specs=in_specs,
        out_specs=out_specs,
        out_shape=out_shape,
        scratch_shapes=[pltpu.VMEM((3, tq, tq + 2 * REACH), F32)],
        compiler_params=_params("arbitrary", "arbitrary", "arbitrary"),
        name=f"dilated_attn_g{g}",
    )(*args)
    if last:
        return res
    return res[0].reshape(A_HEADS, S, hd), res[1].reshape(A_HEADS, S, hd)


def _diff_kernel(tab_ref, dl_ref, g_ref, q_ref, k_ref, v_ref, gate_ref, o_ref, strip_ref, *,
                 lam_init, S, tq, tk, dmin, dmax):
    h, i = pl.program_id(0), pl.program_id(1)
    width = strip_ref.shape[1]

    @pl.when(i == 0)
    def _():
        t = lax.broadcasted_iota(jnp.int32, (tq, width), 0)
        c = lax.broadcasted_iota(jnp.int32, (tq, width), 1)
        strip_ref[...] = _rel_bias(c - t + dmin, tab_ref, 3 * A_HEADS + h)

    q = q_ref[...] * (1.0 / math.sqrt(C_QK_DIM))
    lane = lax.broadcasted_iota(jnp.int32, q.shape, 1)
    zero = jnp.zeros_like(q)
    q2 = jnp.concatenate([jnp.where(lane < C_QK_DIM, q, zero), jnp.where(lane >= C_QK_DIM, q, zero)], axis=0)

    def body(j, carry):
        m, l, acc = carry
        k0 = pl.multiple_of(j * tk, tk)
        kt = k_ref[pl.ds(k0, tk), :]
        vt = v_ref[pl.ds(k0, tk), :]
        s = lax.dot_general(q2, kt, (((1,), (1,)), ((), ())), preferred_element_type=F32)
        d = jnp.clip(j * tk - i * tq, dmin, dmax)
        b = strip_ref[:, pl.ds(pl.multiple_of(d - dmin, LANES), tk)]
        s = s + jnp.concatenate([b, b], axis=0)
        m_new = jnp.maximum(m, jnp.max(s, axis=-1, keepdims=True))
        alpha = jnp.exp(m - m_new)
        p = jnp.exp(s - m_new)
        l = alpha * l + jnp.sum(p, axis=-1, keepdims=True)
        acc = alpha * acc + jnp.dot(p.astype(BF16), vt, preferred_element_type=F32)
        return m_new, l, acc

    init = (jnp.full((2 * tq, 1), NEG, F32), jnp.zeros((2 * tq, 1), F32), jnp.zeros((2 * tq, C_V_DIM), F32))
    _, l, acc = lax.fori_loop(0, S // tk, body, init)
    on = acc / l
    dl = dl_ref[...]
    lam = (jnp.exp(jnp.sum(dl[0:1] * dl[1:2], axis=-1, keepdims=True))
           - jnp.exp(jnp.sum(dl[2:3] * dl[3:4], axis=-1, keepdims=True)) + lam_init)
    o = on[:tq] - lam * on[tq:]
    y = o * lax.rsqrt(jnp.mean(o * o, axis=-1, keepdims=True) + NORM_EPS) * g_ref[...]
    o_ref[...] = (y * (1.0 - lam_init) * _silu(gate_ref[...])).astype(o_ref.dtype)


def _diff_attention(tab, cqkv, diff_lam, diff_g, layer, gate_src, gate_col, tq=256, tk=512):
    nh, S, hd = cqkv.shape
    assert S % tk == 0 and S % tq == 0
    lam_init = 0.8 - 0.6 * math.exp(-0.3 * layer)
    dmax = pl.cdiv(FAR_DIST + tq - 1, LANES) * LANES
    dmin = -pl.cdiv(FAR_DIST + tk - 1, LANES) * LANES
    width = dmax - dmin + tk
    gcb = gate_col // hd
    return pl.pallas_call(
        functools.partial(_diff_kernel, lam_init=lam_init, S=S, tq=tq, tk=tk, dmin=dmin, dmax=dmax),
        grid=(C_HEADS, S // tq),
        in_specs=[pl.BlockSpec(memory_space=pltpu.SMEM),
                  pl.BlockSpec((None, 4, C_QK_DIM), lambda h, i: (layer, 0, 0)),
                  pl.BlockSpec((None, 1, C_V_DIM), lambda h, i: (layer, 0, 0)),
                  pl.BlockSpec((None, tq, hd), lambda h, i: (h, i, 0)),
                  pl.BlockSpec((None, S, hd), lambda h, i: (C_HEADS + h, 0, 0)),
                  pl.BlockSpec((None, S, hd), lambda h, i: (2 * C_HEADS + h, 0, 0)),
                  pl.BlockSpec((tq, hd), lambda h, i: (i, gcb + h))],
        out_specs=pl.BlockSpec((tq, hd), lambda h, i: (i, h)),
        out_shape=jax.ShapeDtypeStruct((S, C_HEADS * hd), BF16),
        scratch_shapes=[pltpu.VMEM((tq, width), F32)],
        compiler_params=_params("arbitrary", "arbitrary"),
        name="diff_attn",
    )(tab, diff_lam, diff_g.reshape(diff_g.shape[0], 1, C_V_DIM), cqkv, cqkv, cqkv, gate_src)


def _short_conv_kernel(b_ref, w_ref, o_ref):
    b = b_ref[...]
    S = b.shape[0]
    row = lax.broadcasted_iota(jnp.int32, b.shape, 0)
    prev = jnp.where(row == 0, 0.0, pltpu.roll(b, 1, 0))
    nxt = jnp.where(row == S - 1, 0.0, pltpu.roll(b, S - 1, 0))
    w = w_ref[...]
    o_ref[...] = w[0:1] * prev + w[1:2] * b + w[2:3] * nxt


def _short_conv(p1, hy_conv, layer, col0):
    S = p1.shape[0]
    nct = BRANCH_W // LANES
    cb = col0 // LANES
    return pl.pallas_call(
        _short_conv_kernel,
        grid=(3 * nct,),
        in_specs=[pl.BlockSpec((S, LANES), lambda c: (0, cb + c)),
                  pl.BlockSpec((None, 3, LANES), lambda c: (layer, 0, c))],
        out_specs=pl.BlockSpec((None, S, LANES), lambda c: (c // nct, 0, c % nct)),
        out_shape=jax.ShapeDtypeStruct((3, S, BRANCH_W), F32),
        compiler_params=_params("parallel"),
        name="short_conv",
    )(p1, hy_conv)


def _filter_kernel(w1t_ref, w1c_ref, w1s_ref, b1_ref, fr_ref, w2_ref, b2_ref, w3_ref, w3b_ref, skip_ref,
                   ad_ref, o_ref, *, L, tr):
    j = pl.program_id(1)
    N = 2 * L

    def mlp(pos, w3):
        t = pos * (1.0 / (L - 1))
        w = pos * (2.0 * math.pi / L)
        fb = (1e-4 + lax.broadcasted_iota(jnp.int32, (1, HY_BANDS), 1).astype(F32)
              * ((HY_BANDS - 1 - 1e-4) / (HY_BANDS - 1)))
        ang = w * fb
        pre = (t * w1t_ref[...]
               + jnp.dot(jnp.cos(ang), w1c_ref[...], precision=HIGHEST, preferred_element_type=F32)
               - jnp.dot(jnp.sin(ang), w1s_ref[...], precision=HIGHEST, preferred_element_type=F32)
               + b1_ref[...])
        fr = fr_ref[...]
        hid = jnp.sin(fr[0:1] * pre)
        hid = jnp.sin(fr[1:2] * (jnp.dot(hid, w2_ref[...], precision=HIGHEST, preferred_element_type=F32)
                                  + b2_ref[...]))
        filt = jnp.dot(hid, w3, precision=HIGHEST, preferred_element_type=F32)
        return filt * jnp.exp(-t * ad_ref[...])

    row = j * tr + lax.broadcasted_iota(jnp.int32, (tr, 1), 0)
    pos = jnp.where(row < L, row, N - row)
    g = mlp(pos.astype(F32), w3_ref[...])
    o_ref[...] = jnp.where(row == L, 0.0, g)

    @pl.when(j == 0)
    def _():
        hb0 = mlp(jnp.zeros((8, 1), F32), w3b_ref[...])
        head = o_ref[0:8, :]
        r8 = lax.broadcasted_iota(jnp.int32, head.shape, 0)
        o_ref[0:8, :] = head + jnp.where(r8 == 0, hb0 + skip_ref[...], 0.0)


def _hyena_filters(L, hy_w1, hy_b1, hy_freq, hy_w2, hy_b2, hy_w3, hy_skip, layer, tr=512):
    N = 2 * L
    W = BRANCH_W
    nb = HY_BANDS
    w1 = hy_w1[layer]
    max_decay = math.log(HY_TARGET) / HY_FAST_PCT
    min_decay = math.log(HY_TARGET) / HY_SLOW_PCT
    absdelta = jnp.abs(jnp.linspace(min_decay, max_decay, W, dtype=F32)).reshape(1, W)
    full = lambda shape: pl.BlockSpec(shape, lambda o, j: (0,) * len(shape))
    nfwd = L // tr
    return pl.pallas_call(
        functools.partial(_filter_kernel, L=L, tr=tr),
        grid=(HY_ORDER, N // tr),
        in_specs=[full((1, HY_FFN)), full((nb, HY_FFN)), full((nb, HY_FFN)), full((1, HY_FFN)), full((2, HY_FFN)),
                  full((HY_FFN, HY_FFN)), full((1, HY_FFN)),
                  pl.BlockSpec((None, HY_FFN, W), lambda o, j: (layer, 0, 2 * o + (j >= nfwd).astype(jnp.int32))),
                  pl.BlockSpec((None, HY_FFN, W), lambda o, j: (layer, 0, 2 * o + 1)),
                  pl.BlockSpec((None, None, 1, W), lambda o, j: (layer, o, 0, 0)),
                  full((1, W))],
        out_specs=pl.BlockSpec((None, tr, W), lambda o, j: (o, j, 0)),
        out_shape=jax.ShapeDtypeStruct((HY_ORDER, N, W), F32),
        compiler_params=_params("parallel", "arbitrary"),
        name="hyena_filter",
    )(w1[0:1], w1[1:1 + nb], w1[1 + nb:], hy_b1[layer].reshape(1, HY_FFN), hy_freq[layer], hy_w2[layer],
      hy_b2[layer].reshape(1, HY_FFN), hy_w3, hy_w3, hy_skip.reshape(hy_skip.shape[0], HY_ORDER, 1, W), absdelta)


def _lmm_kernel(f_ref, x_ref, *rest, gated):
    o_ref = rest[-1]
    y = jnp.dot(f_ref[...], x_ref[...], precision=HIGHEST, preferred_element_type=F32)
    if gated:
        y = y * rest[0][...]
    o_ref[...] = y


def _left_matmul(f, x, gate=None, tc=2048):
    B, K, C = x.shape
    M = f.shape[0]
    in_specs = [pl.BlockSpec((M, K), lambda b, c: (0, 0)), pl.BlockSpec((None, K, tc), lambda b, c: (b, 0, c))]
    args = [f, x]
    if gate is not None:
        in_specs.append(pl.BlockSpec((None, M, tc), lambda b, c: (b, 0, c)))
        args.append(gate)
    return pl.pallas_call(
        functools.partial(_lmm_kernel, gated=gate is not None),
        grid=(B, C // tc),
        in_specs=in_specs,
        out_specs=pl.BlockSpec((None, M, tc), lambda b, c: (b, 0, c)),
        out_shape=jax.ShapeDtypeStruct((B, M, C), F32),
        compiler_params=_params("parallel", "parallel"),
        name="dft_outer",
    )(*args)


def _spectrum_kernel(gf_ref, a_ref, o_ref):
    o_ref[...] = jnp.dot(gf_ref[...], a_ref[...], precision=HIGHEST, preferred_element_type=F32)


def _filter_spectrum(gf, a):
    O, N1, R, W = a.shape
    return pl.pallas_call(
        _spectrum_kernel,
        grid=(O, N1),
        in_specs=[pl.BlockSpec((None, R, R), lambda o, k: (k, 0, 0)),
                  pl.BlockSpec((None, None, R, W), lambda o, k: (o, k, 0, 0))],
        out_specs=pl.BlockSpec((None, None, R, W), lambda o, k: (o, k, 0, 0)),
        out_shape=jax.ShapeDtypeStruct(a.shape, F32),
        compiler_params=_params("parallel", "parallel"),
        name="filter_spectrum",
    )(gf, a)


def _conv_mid_kernel(gf_ref, gi_ref, a_ref, h_ref, o_ref):
    half = a_ref.shape[0] // 2
    x = jnp.dot(gf_ref[...], a_ref[...], precision=HIGHEST, preferred_element_type=F32)
    xr, xi = x[:half], x[half:]
    hr, hi = h_ref[:half, :], h_ref[half:, :]
    y = jnp.concatenate([xr * hr - xi * hi, xr * hi + xi * hr], axis=0)
    o_ref[...] = jnp.dot(gi_ref[...], y, precision=HIGHEST, preferred_element_type=F32)


def _conv_mid(gf, gi, a, hspec, order):
    N1, R, W = a.shape
    return pl.pallas_call(
        _conv_mid_kernel,
        grid=(N1,),
        in_specs=[pl.BlockSpec((None, R, R), lambda k: (k, 0, 0)),
                  pl.BlockSpec((None, R, R), lambda k: (k, 0, 0)),
                  pl.BlockSpec((None, R, W), lambda k: (k, 0, 0)),
                  pl.BlockSpec((None, None, R, W), lambda k: (order, k, 0, 0))],
        out_specs=pl.BlockSpec((None, R, W), lambda k: (k, 0, 0)),
        out_shape=jax.ShapeDtypeStruct(a.shape, F32),
        compiler_params=_params("parallel"),
        name="conv_mid",
    )(gf, gi, a, hspec)


def _dft_tables(L):
    N = 2 * L
    N2 = LANES
    N1 = N // N2
    two_pi = 2.0 * math.pi
    k1 = jnp.arange(N1, dtype=jnp.int32)
    ang = ((k1[:, None] * k1[None, :]) % N1).astype(F32) * (two_pi / N1)
    fa = jnp.stack([jnp.cos(ang), -jnp.sin(ang)], axis=1).reshape(2 * N1, N1)
    fi = (jnp.stack([jnp.cos(ang), -jnp.sin(ang)], axis=2).reshape(N1, 2 * N1) / N)[:N1 // 2]
    t2 = jnp.arange(N2, dtype=jnp.int32)
    k = k1[:, None, None] + N1 * t2[None, :, None]
    ph = ((k * t2[None, None, :]) % N).astype(F32) * (two_pi / N)
    gr, gim = jnp.cos(ph), -jnp.sin(ph)
    gf = jnp.concatenate([jnp.concatenate([gr, -gim], axis=2), jnp.concatenate([gim, gr], axis=2)], axis=1)
    gi = jnp.swapaxes(gf, 1, 2)
    return fa, fi, gf, gi


def _long_conv(z, hspec, order, tables, gate):
    fa, fi, gf, gi = tables
    S, W = z.shape
    N1 = fa.shape[1]
    N2 = 2 * S // N1
    a = _left_matmul(fa[:, :N1 // 2], z.reshape(1, N1 // 2, N2 * W))
    b = _conv_mid(gf, gi, a.reshape(N1, 2 * N2, W), hspec, order)
    y = _left_matmul(fi, b.reshape(1, 2 * N1, N2 * W), gate.reshape(1, N1 // 2, N2 * W))
    return y.reshape(S, W)


def _merge_kernel(a_ref, zb_ref, bg_ref, c_ref, w_ref, ma_ref, mb_ref, mc_ref, bias_ref, o_ref):
    b = (zb_ref[...] * _silu(bg_ref[...])).astype(BF16)
    bias = bias_ref[...]
    y = None
    for n, (br, m_ref) in enumerate(((a_ref[...], ma_ref), (b, mb_ref), (c_ref[...], mc_ref))):
        pb = jnp.dot(br, w_ref[n], preferred_element_type=F32)
        gate = 1.0 / (1.0 + jnp.exp(-(m_ref[...] + bias[n:n + 1])))
        y = gate * pb if y is None else y + gate * pb
    o_ref[...] = y.astype(o_ref.dtype)


def _merge(a_out, zb, p1, c_out, p2, w_proj_bf, merge_b, layer, tm=512, tn=512):
    S = a_out.shape[0]
    W, D = BRANCH_W, D_MODEL
    bgb = (P1_COLS - W) // W
    mcb = W // tn
    branch = pl.BlockSpec((tm, W), lambda i, j: (i, 0))
    mspec = lambda n: pl.BlockSpec((tm, tn), lambda i, j: (i, mcb + n * (D // tn) + j))
    return pl.pallas_call(
        _merge_kernel,
        grid=(S // tm, D // tn),
        in_specs=[branch, branch, pl.BlockSpec((tm, W), lambda i, j: (i, bgb)), branch,
                  pl.BlockSpec((None, N_BRANCH, W, tn), lambda i, j: (layer, 0, 0, j)),
                  mspec(0), mspec(1), mspec(2),
                  pl.BlockSpec((None, N_BRANCH, tn), lambda i, j: (layer, 0, j))],
        out_specs=pl.BlockSpec((tm, tn), lambda i, j: (i, j)),
        out_shape=jax.ShapeDtypeStruct((S, D), BF16),
        compiler_params=_params("parallel", "arbitrary"),
        name="merge",
    )(a_out, zb, p1, c_out, w_proj_bf, p2, p2, p2, merge_b)


def _out_kernel(x_ref, y_ref, w_ref, g_ref, xo_ref, ho_ref):
    x = x_ref[...] + jnp.dot(y_ref[...], w_ref[...], preferred_element_type=F32)
    xo_ref[...] = x
    ms = jnp.mean(x * x, axis=-1, keepdims=True)
    ho_ref[...] = (x * lax.rsqrt(ms + NORM_EPS) * g_ref[...]).astype(ho_ref.dtype)


def _out_proj(x, y, w_out_bf, layer, g, h_dtype, tm=512):
    S, D = x.shape
    row = pl.BlockSpec((tm, D), lambda i: (i, 0))
    return pl.pallas_call(
        _out_kernel,
        grid=(S // tm,),
        in_specs=[row, row, pl.BlockSpec((None, D, D), lambda i: (layer, 0, 0)), pl.BlockSpec((1, D), lambda i: (0, 0))],
        out_specs=(row, row),
        out_shape=(jax.ShapeDtypeStruct((S, D), F32), jax.ShapeDtypeStruct((S, D), h_dtype)),
        compiler_params=_params("parallel"),
        name="out_proj",
    )(x, y, w_out_bf, g.reshape(1, D))


def kernel(x, norm_g, final_g, w_in, merge_b, rel_bias, hy_conv, hy_w1, hy_b1, hy_freq, hy_w2, hy_b2, hy_w3,
           hy_skip, diff_lam, diff_g, w_proj, w_out):
    B, S, D = x.shape
    depth = w_in.shape[0]
    w_proj_bf = w_proj.astype(BF16)
    w_out_bf = w_out.astype(BF16)
    tables = _dft_tables(S)
    fa, _, gf, _ = tables
    N1 = fa.shape[1]
    N2 = 2 * S // N1
    outs = []
    for b in range(B):
        xb = x[b]
        h = _rmsnorm(xb, norm_g[0], BF16)
        for l in range(depth):
            aqkv = _in_proj(h, w_in, l, COL_A, A_QKV_COLS, heads=True)
            p1 = _in_proj(h, w_in, l, COL_P1, P1_COLS, heads=False)
            cqkv = _in_proj(h, w_in, l, COL_C, C_QKV_COLS, heads=True)
            p2 = _in_proj(h, w_in, l, COL_P2, P2_COLS, heads=False)

            state = _dilated_group(rel_bias, aqkv, 2, None, None)
            state = _dilated_group(rel_bias, aqkv, 1, state, None)
            a_out = _dilated_group(rel_bias, aqkv, 0, state, p1)

            u = _short_conv(p1, hy_conv, l, BRANCH_W)
            filt = _hyena_filters(S, hy_w1, hy_b1, hy_freq, hy_w2, hy_b2, hy_w3, hy_skip, l)
            fa_out = _left_matmul(fa, filt.reshape(HY_ORDER, N1, N2 * BRANCH_W))
            hspec = _filter_spectrum(gf, fa_out.reshape(HY_ORDER, N1, 2 * N2, BRANCH_W))
            z = _long_conv(u[0], hspec, 0, tables, u[1])
            zb = _long_conv(z, hspec, 1, tables, u[2])

            c_out = _diff_attention(rel_bias, cqkv, diff_lam, diff_g, l, p2, 0)

            y = _merge(a_out, zb, p1, c_out, p2, w_proj_bf, merge_b, l)
            if l + 1 < depth:
                xb, h = _out_proj(xb, y, w_out_bf, l, norm_g[l + 1], BF16)
            else:
                _, h = _out_proj(xb, y, w_out_bf, l, final_g, F32)
        outs.append(h)
    return jnp.stack(outs)
```

```python
import functools
import math

import jax
import jax.numpy as jnp
from jax import lax
from jax.experimental import pallas as pl
from jax.experimental.pallas import tpu as pltpu

F32 = jnp.float32
BF16 = jnp.bfloat16

LANES = 128
SUBLANES = 8
D_MODEL = 2048
BRANCH_W = 1024
N_BRANCH = 3
DIL_PAIRS = ((128, 1), (512, 4), (2048, 16))
A_HEADS = 8
A_HEAD_DIM = 128
REACH = 64
A_SUB = 128
A_KW = A_SUB + 2 * REACH
C_HEADS = 8
C_QK_DIM = 64
C_V_DIM = 128
HY_BANDS = 16
HY_FFN = 64
HY_ORDER = 2
HY_TARGET = 1e-2
HY_FAST_PCT = 0.3
HY_SLOW_PCT = 1.5
N_BUCKETS = 32
REL_MAX_DIST = 1024
NORM_EPS = 1e-6
NEG = -1e30

A_GROUP_COLS = 3 * A_HEADS * A_HEAD_DIM
A_QKV_COLS = 3 * A_GROUP_COLS
P1_COLS = BRANCH_W + 3 * BRANCH_W + BRANCH_W
C_QKV_COLS = 3 * C_HEADS * C_V_DIM
P2_COLS = BRANCH_W + N_BRANCH * D_MODEL
COL_A, COL_P1, COL_C, COL_P2 = 0, A_QKV_COLS, A_QKV_COLS + P1_COLS, A_QKV_COLS + P1_COLS + C_QKV_COLS

VMEM_LIMIT = 56 * 1024 * 1024
HIGHEST = lax.Precision.HIGHEST
LOG2E = math.log2(math.e)
C_Q_SCALE = LOG2E / math.sqrt(C_QK_DIM)


def _bucket_thresholds():
    nb = N_BUCKETS // 2
    max_exact = nb // 2
    thr = []
    for k in range(1, nb - max_exact):
        n = max_exact
        while int(math.log(n / max_exact) / math.log(REL_MAX_DIST / max_exact) * (nb - max_exact)) < k:
            n += 1
        thr.append(n)
    return tuple(thr)


BUCKET_THR = _bucket_thresholds()
FAR_DIST = BUCKET_THR[-1]


def _params(*sem):
    return pltpu.CompilerParams(dimension_semantics=sem, vmem_limit_bytes=VMEM_LIMIT)


def _rel_bias(rel, tab_ref, col):
    nb = N_BUCKETS // 2
    n = jnp.abs(rel)
    large = jnp.full(rel.shape, nb // 2, jnp.int32)
    for thr in BUCKET_THR:
        large = large + (n >= thr).astype(jnp.int32)
    bucket = jnp.where(n < nb // 2, n, large) + jnp.where(rel > 0, nb, 0)
    out = jnp.full(rel.shape, tab_ref[0, col], F32)
    for j in range(1, N_BUCKETS):
        out = jnp.where(bucket == j, tab_ref[j, col], out)
    return out


def _silu(x):
    return x * (1.0 / (1.0 + jnp.exp(-x)))


def _split_bf16(x):
    hi = x.astype(BF16)
    return hi, (x - hi.astype(F32)).astype(BF16)


def _const3(a):
    hi, lo = _split_bf16(a)
    return jnp.concatenate([hi, hi, lo], axis=-1)


def _dot3(a3, x):
    hi, lo = _split_bf16(x)
    return jnp.dot(a3, jnp.concatenate([hi, lo, hi], axis=0), preferred_element_type=F32)


def _rmsnorm_kernel(x_ref, g_ref, o_ref):
    x = x_ref[...]
    ms = jnp.mean(x * x, axis=-1, keepdims=True)
    o_ref[...] = (x * lax.rsqrt(ms + NORM_EPS) * g_ref[...]).astype(o_ref.dtype)


def _rmsnorm(x, g, out_dtype, tm=512):
    S, D = x.shape
    return pl.pallas_call(
        _rmsnorm_kernel,
        grid=(S // tm,),
        in_specs=[pl.BlockSpec((tm, D), lambda i: (i, 0)), pl.BlockSpec((1, D), lambda i: (0, 0))],
        out_specs=pl.BlockSpec((tm, D), lambda i: (i, 0)),
        out_shape=jax.ShapeDtypeStruct((S, D), out_dtype),
        compiler_params=_params("parallel"),
        name="rmsnorm",
    )(x, g.reshape(1, D))


def _proj_kernel(h_ref, w_ref, o_ref, wb_ref, *acc_ref, dil, scale0):
    @pl.when(pl.program_id(1) == 0)
    def _():
        wb_ref[...] = w_ref[...].astype(BF16)

    acc = jnp.dot(h_ref[...], wb_ref[...], preferred_element_type=F32)
    if scale0 is not None:
        acc = acc * jnp.where(pl.program_id(0) == 0, scale0, 1.0)
    tm, tn = acc.shape
    if dil is None:
        o_ref[...] = acc
    elif dil == 1:
        for c in range(tn // LANES):
            o_ref[c, 0] = acc[:, c * LANES:(c + 1) * LANES].astype(o_ref.dtype)
    else:
        for c in range(tn // LANES):
            acc_ref[0][c] = acc[:, c * LANES:(c + 1) * LANES]
            for r in range(dil):
                o_ref[c, r] = acc_ref[0][c, pl.ds(r, tm // dil, stride=dil), :].astype(o_ref.dtype)


def _in_proj(h, w_in, layer, col0, ncols, dil=None, scale0=None, tm=1024, tn=1024):
    S, D = h.shape
    assert col0 % tn == 0 and ncols % tn == 0
    jb = col0 // tn
    scratch = [pltpu.VMEM((D, tn), BF16)]
    if dil is None:
        out_shape = jax.ShapeDtypeStruct((S, ncols), F32)
        out_spec = pl.BlockSpec((tm, tn), lambda j, i: (i, j))
    else:
        out_shape = jax.ShapeDtypeStruct((ncols // LANES, dil, S // dil, LANES), BF16)
        out_spec = pl.BlockSpec((tn // LANES, dil, tm // dil, LANES), lambda j, i: (j, 0, i, 0))
        if dil > 1:
            scratch.append(pltpu.VMEM((tn // LANES, tm, LANES), F32))
    return pl.pallas_call(
        functools.partial(_proj_kernel, dil=dil, scale0=scale0),
        grid=(ncols // tn, S // tm),
        in_specs=[pl.BlockSpec((tm, D), lambda j, i: (i, 0)),
                  pl.BlockSpec((None, D, tn), lambda j, i: (layer, 0, jb + j))],
        out_specs=out_spec,
        out_shape=out_shape,
        scratch_shapes=scratch,
        compiler_params=_params("parallel", "arbitrary"),
        name="in_proj",
    )(h, w_in)


def _dil_kernel(*refs, dil, col0, others, n, tq):
    it = iter(refs)
    tab_ref, q_ref, k_ref, v_ref = next(it), next(it), next(it), next(it)
    other_refs = [(next(it), next(it)) for _ in others]
    gate_ref = next(it) if others else None
    o_ref = next(it)
    lse_ref = None if others else next(it)
    bm_ref = next(it)
    pos_refs = [(next(it), next(it)) for _ in others]

    h, r, i = pl.program_id(0), pl.program_id(1), pl.program_id(2)
    deltas = (-REACH, 0, -2 * REACH)

    @pl.when((r == 0) & (i == 0))
    def _():
        t = lax.broadcasted_iota(jnp.int32, (A_SUB, A_KW), 0)
        u = lax.broadcasted_iota(jnp.int32, (A_SUB, A_KW), 1)
        for idx, delta in enumerate(deltas):
            rel = u + delta - t
            bias = _rel_bias(rel * dil, tab_ref, col0 + h)
            bm_ref[idx] = jnp.where(jnp.abs(rel) <= REACH, bias, NEG)

    for (src_o, src_l), (dst_o, dst_l), d in zip(other_refs, pos_refs, others):
        for rr in range(d):
            dst_o[pl.ds(rr, tq // d, stride=d), :] = src_o[rr]
            dst_l[pl.ds(rr, tq // d, stride=d), :] = src_l[rr]

    for sb in range(tq // A_SUB):
        rows = slice(sb * A_SUB, (sb + 1) * A_SUB)
        m0 = i * tq + sb * A_SUB
        ks = pl.multiple_of(jnp.clip(m0 - REACH, 0, n - A_KW), REACH)
        sel = jnp.where(m0 == 0, 1, jnp.where(m0 == n - A_SUB, 2, 0))
        k = k_ref[pl.ds(ks, A_KW), :]
        v = v_ref[pl.ds(ks, A_KW), :]
        s = lax.dot_general(q_ref[rows, :], k, (((1,), (1,)), ((), ())), preferred_element_type=F32)
        s = s * (1.0 / math.sqrt(A_HEAD_DIM)) + bm_ref[sel]
        m = jnp.max(s, axis=-1, keepdims=True)
        p = jnp.exp(s - m)
        l = jnp.sum(p, axis=-1, keepdims=True)
        o = jnp.dot(p.astype(BF16), v, preferred_element_type=F32) / l
        lse = jnp.broadcast_to(m + jnp.log(l), o.shape)
        if others:
            lses = [lse] + [dst_l[rows, :] for _, dst_l in pos_refs]
            outs = [o] + [dst_o[rows, :] for dst_o, _ in pos_refs]
            mx = functools.reduce(jnp.maximum, lses)
            ws = [jnp.exp(x - mx) for x in lses]
            o = sum(w * x for w, x in zip(ws, outs)) / sum(ws)
            o_ref[rows, :] = (o * _silu(gate_ref[rows, :])).astype(o_ref.dtype)
        else:
            o_ref[rows, :] = o
            lse_ref[rows, :] = lse


def _dilated_group(tab, qkv, g, others, gate_src, tq=512):
    win, dil = DIL_PAIRS[g]
    assert win // (2 * dil) == REACH
    _, _, n, hd = qkv.shape
    tq = min(tq, n)
    assert qkv.shape[1] == dil and n % tq == 0 and n >= A_KW
    last = others is not None
    qblk = pl.BlockSpec((None, None, tq, hd), lambda h, r, i: (h, r, i, 0))
    in_specs = [pl.BlockSpec(memory_space=pltpu.SMEM), qblk,
                pl.BlockSpec((None, None, n, hd), lambda h, r, i: (A_HEADS + h, r, 0, 0)),
                pl.BlockSpec((None, None, n, hd), lambda h, r, i: (2 * A_HEADS + h, r, 0, 0))]
    args = [tab, qkv, qkv, qkv]
    scratch = [pltpu.VMEM((3, A_SUB, A_KW), F32)]
    other_dils = ()
    if last:
        assert dil == 1
        other_dils = tuple(o.shape[1] for o, _ in others)
        for (o, lse), d in zip(others, other_dils):
            spec = pl.BlockSpec((None, d, tq // d, hd), lambda h, r, i: (h, 0, i, 0))
            in_specs += [spec, spec]
            args += [o, lse]
            scratch += [pltpu.VMEM((tq, hd), F32), pltpu.VMEM((tq, hd), F32)]
        in_specs.append(pl.BlockSpec((tq, hd), lambda h, r, i: (i, h)))
        args.append(gate_src)
        out_shape = jax.ShapeDtypeStruct((n, A_HEADS * hd), BF16)
        out_specs = pl.BlockSpec((tq, hd), lambda h, r, i: (i, h))
    else:
        out_shape = (jax.ShapeDtypeStruct((A_HEADS, dil, n, hd), F32),) * 2
        out_specs = (qblk, qblk)
    return pl.pallas_call(
        functools.partial(_dil_kernel, dil=dil, col0=g * A_HEADS, others=other_dils, n=n, tq=tq),
        grid=(A_HEADS, dil, n // tq),
        in_specs=in_specs,
        out_specs=out_specs,
        out_shape=out_shape,
        scratch_shapes=scratch,
        compiler_params=_params("arbitrary", "arbitrary", "arbitrary"),
        name=f"dilated_attn_g{g}",
    )(*args)


def _diff_kernel(tab_ref, dl_ref, g_ref, q_ref, k_ref, v_ref, gate_ref, o_ref, strip_ref, *,
                 lam_init, S, tq, tk, dmin, dmax):
    h, i = pl.program_id(0), pl.program_id(1)
    width = strip_ref.shape[1]
    nk = S // tk

    @pl.when(i == 0)
    def _():
        t = lax.broadcasted_iota(jnp.int32, (tq, width), 0)
        c = lax.broadcasted_iota(jnp.int32, (tq, width), 1)
        strip_ref[...] = _rel_bias(c - t + dmin, tab_ref, 3 * A_HEADS + h) * LOG2E

    q = q_ref[...]
    lane = lax.broadcasted_iota(jnp.int32, q.shape, 1)
    zero = jnp.zeros_like(q)
    q2 = jnp.concatenate([jnp.where(lane < C_QK_DIM, q, zero), jnp.where(lane >= C_QK_DIM, q, zero)], axis=0)

    def body(j, carry):
        m, l, acc = carry
        k0 = pl.multiple_of(j * tk, tk)
        kt = k_ref[pl.ds(k0, tk), :]
        vt = v_ref[pl.ds(k0, tk), :]
        s = lax.dot_general(q2, kt, (((1,), (1,)), ((), ())), preferred_element_type=F32)
        d = jnp.clip(j * tk - i * tq, dmin, dmax)
        b = strip_ref[:, pl.ds(pl.multiple_of(d - dmin, LANES), tk)]
        s = s + jnp.concatenate([b, b], axis=0)
        m_new = jnp.maximum(m, jnp.max(s, axis=-1, keepdims=True))
        alpha = jnp.exp2(m - m_new)
        p = jnp.exp2(s - m_new)
        l = alpha * l + jnp.sum(p, axis=-1, keepdims=True)
        acc = alpha * acc + jnp.dot(p.astype(BF16), vt, preferred_element_type=F32)
        return m_new, l, acc

    init = (jnp.full((2 * tq, 1), NEG, F32), jnp.zeros((2 * tq, 1), F32), jnp.zeros((2 * tq, C_V_DIM), F32))
    _, l, acc = lax.fori_loop(0, nk, body, init)
    on = acc / l
    dl = dl_ref[...]
    lam = (jnp.exp(jnp.sum(dl[0:1] * dl[1:2], axis=-1, keepdims=True))
           - jnp.exp(jnp.sum(dl[2:3] * dl[3:4], axis=-1, keepdims=True)) + lam_init)
    o = on[:tq] - lam * on[tq:]
    y = o * lax.rsqrt(jnp.mean(o * o, axis=-1, keepdims=True) + NORM_EPS) * g_ref[...]
    o_ref[...] = (y * (1.0 - lam_init) * _silu(gate_ref[...])).astype(o_ref.dtype)


def _diff_attention(tab, cqkv, diff_lam, diff_g, layer, gate_src, gate_col, tq=512, tk=1024):
    _, _, S, hd = cqkv.shape
    assert S % tk == 0 and S % tq == 0
    lam_init = 0.8 - 0.6 * math.exp(-0.3 * layer)
    dmax = pl.cdiv(FAR_DIST + tq - 1, LANES) * LANES
    dmin = -pl.cdiv(FAR_DIST + tk - 1, LANES) * LANES
    width = dmax - dmin + tk
    gcb = gate_col // hd
    return pl.pallas_call(
        functools.partial(_diff_kernel, lam_init=lam_init, S=S, tq=tq, tk=tk, dmin=dmin, dmax=dmax),
        grid=(C_HEADS, S // tq),
        in_specs=[pl.BlockSpec(memory_space=pltpu.SMEM),
                  pl.BlockSpec((None, 4, C_QK_DIM), lambda h, i: (layer, 0, 0)),
                  pl.BlockSpec((None, 1, C_V_DIM), lambda h, i: (layer, 0, 0)),
                  pl.BlockSpec((None, None, tq, hd), lambda h, i: (h, 0, i, 0)),
                  pl.BlockSpec((None, None, S, hd), lambda h, i: (C_HEADS + h, 0, 0, 0)),
                  pl.BlockSpec((None, None, S, hd), lambda h, i: (2 * C_HEADS + h, 0, 0, 0)),
                  pl.BlockSpec((tq, hd), lambda h, i: (i, gcb + h))],
        out_specs=pl.BlockSpec((tq, hd), lambda h, i: (i, h)),
        out_shape=jax.ShapeDtypeStruct((S, C_HEADS * hd), BF16),
        scratch_shapes=[pltpu.VMEM((tq, width), F32)],
        compiler_params=_params("arbitrary", "arbitrary"),
        name="diff_attn",
    )(tab, diff_lam, diff_g.reshape(diff_g.shape[0], 1, C_V_DIM), cqkv, cqkv, cqkv, gate_src)


def _short_conv_kernel(b_ref, w_ref, o_ref):
    b = b_ref[...]
    S = b.shape[0]
    row = lax.broadcasted_iota(jnp.int32, b.shape, 0)
    prev = jnp.where(row == 0, 0.0, pltpu.roll(b, 1, 0))
    nxt = jnp.where(row == S - 1, 0.0, pltpu.roll(b, S - 1, 0))
    w = w_ref[...]
    o_ref[...] = w[0:1] * prev + w[1:2] * b + w[2:3] * nxt


def _short_conv(p1, hy_conv, layer, col0):
    S = p1.shape[0]
    nct = BRANCH_W // LANES
    cb = col0 // LANES
    return pl.pallas_call(
        _short_conv_kernel,
        grid=(3 * nct,),
        in_specs=[pl.BlockSpec((S, LANES), lambda c: (0, cb + c)),
                  pl.BlockSpec((None, 3, LANES), lambda c: (layer, 0, c))],
        out_specs=pl.BlockSpec((None, S, LANES), lambda c: (c // nct, 0, c % nct)),
        out_shape=jax.ShapeDtypeStruct((3, S, BRANCH_W), F32),
        compiler_params=_params("parallel"),
        name="short_conv",
    )(p1, hy_conv)


def _filter_kernel(w1t_ref, w1c_ref, w1s_ref, b1_ref, fr_ref, w2_ref, b2_ref, w3_ref, w3b_ref, skip_ref,
                   ad_ref, o_ref, *, L, tr):
    j = pl.program_id(1)
    N = 2 * L

    def mlp(pos, w3):
        t = pos * (1.0 / (L - 1))
        w = pos * (2.0 * math.pi / L)
        fb = (1e-4 + lax.broadcasted_iota(jnp.int32, (1, HY_BANDS), 1).astype(F32)
              * ((HY_BANDS - 1 - 1e-4) / (HY_BANDS - 1)))
        ang = w * fb
        pre = (t * w1t_ref[...]
               + jnp.dot(jnp.cos(ang), w1c_ref[...], precision=HIGHEST, preferred_element_type=F32)
               - jnp.dot(jnp.sin(ang), w1s_ref[...], precision=HIGHEST, preferred_element_type=F32)
               + b1_ref[...])
        fr = fr_ref[...]
        hid = jnp.sin(fr[0:1] * pre)
        hid = jnp.sin(fr[1:2] * (jnp.dot(hid, w2_ref[...], precision=HIGHEST, preferred_element_type=F32)
                                  + b2_ref[...]))
        filt = jnp.dot(hid, w3, precision=HIGHEST, preferred_element_type=F32)
        return filt * jnp.exp(-t * ad_ref[...])

    row = j * tr + lax.broadcasted_iota(jnp.int32, (tr, 1), 0)
    pos = jnp.where(row < L, row, N - row)
    g = mlp(pos.astype(F32), w3_ref[...])
    o_ref[...] = jnp.where(row == L, 0.0, g)

    @pl.when(j == 0)
    def _():
        hb0 = mlp(jnp.zeros((SUBLANES, 1), F32), w3b_ref[...])
        head = o_ref[0:SUBLANES, :]
        r8 = lax.broadcasted_iota(jnp.int32, head.shape, 0)
        o_ref[0:SUBLANES, :] = head + jnp.where(r8 == 0, hb0 + skip_ref[...], 0.0)


def _hyena_filters(L, hy_w1, hy_b1, hy_freq, hy_w2, hy_b2, hy_w3, hy_skip, layer, tr=512):
    N = 2 * L
    W = BRANCH_W
    nb = HY_BANDS
    w1 = hy_w1[layer]
    max_decay = math.log(HY_TARGET) / HY_FAST_PCT
    min_decay = math.log(HY_TARGET) / HY_SLOW_PCT
    absdelta = jnp.abs(jnp.linspace(min_decay, max_decay, W, dtype=F32)).reshape(1, W)
    full = lambda shape: pl.BlockSpec(shape, lambda o, j: (0,) * len(shape))
    nfwd = L // tr
    return pl.pallas_call(
        functools.partial(_filter_kernel, L=L, tr=tr),
        grid=(HY_ORDER, N // tr),
        in_specs=[full((1, HY_FFN)), full((nb, HY_FFN)), full((nb, HY_FFN)), full((1, HY_FFN)), full((2, HY_FFN)),
                  full((HY_FFN, HY_FFN)), full((1, HY_FFN)),
                  pl.BlockSpec((None, HY_FFN, W), lambda o, j: (layer, 0, 2 * o + (j >= nfwd).astype(jnp.int32))),
                  pl.BlockSpec((None, HY_FFN, W), lambda o, j: (layer, 0, 2 * o + 1)),
                  pl.BlockSpec((None, None, 1, W), lambda o, j: (layer, o, 0, 0)),
                  full((1, W))],
        out_specs=pl.BlockSpec((None, tr, W), lambda o, j: (o, j, 0)),
        out_shape=jax.ShapeDtypeStruct((HY_ORDER, N, W), F32),
        compiler_params=_params("parallel", "arbitrary"),
        name="hyena_filter",
    )(w1[0:1], w1[1:1 + nb], w1[1 + nb:], hy_b1[layer].reshape(1, HY_FFN), hy_freq[layer], hy_w2[layer],
      hy_b2[layer].reshape(1, HY_FFN), hy_w3, hy_w3, hy_skip.reshape(hy_skip.shape[0], HY_ORDER, 1, W), absdelta)


def _outer_dft_kernel(f_ref, x_ref, *rest, gated):
    gate_ref = rest[0] if gated else None
    o_ref, xs_ref, ys_ref = rest[-3:]
    K, sub, tw = x_ref.shape
    M = o_ref.shape[0]
    nch = tw // LANES
    for c in range(nch):
        xs_ref[c] = x_ref[:, :, c * LANES:(c + 1) * LANES].reshape(K * sub, LANES)
    f3 = f_ref[...]
    for s in range(sub):
        x = jnp.concatenate([xs_ref[c, pl.ds(s, K, stride=sub), :] for c in range(nch)], axis=1)
        y = _dot3(f3, x)
        for c in range(nch):
            ys_ref[c, pl.ds(s, M, stride=sub), :] = y[:, c * LANES:(c + 1) * LANES]
    for c in range(nch):
        y = ys_ref[c].reshape(M, sub, LANES)
        if gated:
            y = y * gate_ref[:, :, c * LANES:(c + 1) * LANES]
        o_ref[:, :, c * LANES:(c + 1) * LANES] = y


def _outer_dft(f3, x, xb, gate=None, gb=0, tw=BRANCH_W):
    _, K, N2, W = x.shape
    M = f3.shape[0]
    B = x.shape[0] if xb is None else 1
    x_idx = (lambda b: b) if xb is None else (lambda b: xb)
    blk = lambda rows, idx: pl.BlockSpec((None, rows, SUBLANES, tw), lambda b, t, c: (idx(b), 0, t, c))
    in_specs = [pl.BlockSpec((M, 3 * K), lambda b, t, c: (0, 0)), blk(K, x_idx)]
    args = [f3, x]
    if gate is not None:
        in_specs.append(blk(M, lambda b: gb))
        args.append(gate)
    return pl.pallas_call(
        functools.partial(_outer_dft_kernel, gated=gate is not None),
        grid=(B, N2 // SUBLANES, W // tw),
        in_specs=in_specs,
        out_specs=blk(M, lambda b: b),
        out_shape=jax.ShapeDtypeStruct((B, M, N2, W), F32),
        scratch_shapes=[pltpu.VMEM((tw // LANES, K * SUBLANES, LANES), F32),
                        pltpu.VMEM((tw // LANES, M * SUBLANES, LANES), F32)],
        compiler_params=_params("parallel", "parallel", "parallel"),
        name="dft_outer",
    )(*args)


def _spectrum_kernel(gf_ref, a_ref, o_ref):
    o_ref[...] = _dot3(gf_ref[...], a_ref[...])


def _filter_spectrum(gf3, a):
    O, N1, R, W = a.shape
    return pl.pallas_call(
        _spectrum_kernel,
        grid=(O, N1),
        in_specs=[pl.BlockSpec((None, R, 3 * R), lambda o, k: (k, 0, 0)),
                  pl.BlockSpec((None, None, R, W), lambda o, k: (o, k, 0, 0))],
        out_specs=pl.BlockSpec((None, None, R, W), lambda o, k: (o, k, 0, 0)),
        out_shape=jax.ShapeDtypeStruct(a.shape, F32),
        compiler_params=_params("parallel", "parallel"),
        name="filter_spectrum",
    )(gf3, a)


def _conv_mid_kernel(gf_ref, gi_ref, a_ref, h_ref, o_ref):
    half = a_ref.shape[0] // 2
    x = _dot3(gf_ref[...], a_ref[...])
    xr, xi = x[:half], x[half:]
    hr, hi = h_ref[:half, :], h_ref[half:, :]
    y = jnp.concatenate([xr * hr - xi * hi, xr * hi + xi * hr], axis=0)
    o_ref[...] = _dot3(gi_ref[...], y)


def _conv_mid(gf3, gi3, a, hspec, order):
    N1, R, W = a.shape
    return pl.pallas_call(
        _conv_mid_kernel,
        grid=(N1,),
        in_specs=[pl.BlockSpec((None, R, 3 * R), lambda k: (k, 0, 0)),
                  pl.BlockSpec((None, R, 3 * R), lambda k: (k, 0, 0)),
                  pl.BlockSpec((None, R, W), lambda k: (k, 0, 0)),
                  pl.BlockSpec((None, None, R, W), lambda k: (order, k, 0, 0))],
        out_specs=pl.BlockSpec((None, R, W), lambda k: (k, 0, 0)),
        out_shape=jax.ShapeDtypeStruct(a.shape, F32),
        compiler_params=_params("parallel"),
        name="conv_mid",
    )(gf3, gi3, a, hspec)


def _dft_tables(L):
    N = 2 * L
    N2 = LANES
    N1 = N // N2
    two_pi = 2.0 * math.pi
    k1 = jnp.arange(N1, dtype=jnp.int32)
    ang = ((k1[:, None] * k1[None, :]) % N1).astype(F32) * (two_pi / N1)
    fa = jnp.stack([jnp.cos(ang), -jnp.sin(ang)], axis=1).reshape(2 * N1, N1)
    fi = (jnp.stack([jnp.cos(ang), -jnp.sin(ang)], axis=2).reshape(N1, 2 * N1) / N)[:N1 // 2]
    t2 = jnp.arange(N2, dtype=jnp.int32)
    k = k1[:, None, None] + N1 * t2[None, :, None]
    ph = ((k * t2[None, None, :]) % N).astype(F32) * (two_pi / N)
    gr, gim = jnp.cos(ph), -jnp.sin(ph)
    gf = jnp.concatenate([jnp.concatenate([gr, -gim], axis=2), jnp.concatenate([gim, gr], axis=2)], axis=1)
    gi = jnp.swapaxes(gf, 1, 2)
    return {"fa": _const3(fa), "fh": _const3(fa[:, :N1 // 2]), "fi": _const3(fi), "gf": _const3(gf),
            "gi": _const3(gi), "N1": N1, "N2": N2}


def _long_conv(z, zi, gate, gi_, hspec, order, tb):
    _, S, W = z.shape
    N1, N2 = tb["N1"], tb["N2"]
    split = lambda t: t.reshape(t.shape[0], N1 // 2, N2, W)
    a = _outer_dft(tb["fh"], split(z), zi)
    b = _conv_mid(tb["gf"], tb["gi"], a.reshape(N1, 2 * N2, W), hspec, order)
    y = _outer_dft(tb["fi"], b.reshape(1, 2 * N1, N2, W), 0, split(gate), gi_)
    return y.reshape(1, S, W)


def _merge_kernel(a_ref, zb_ref, bg_ref, c_ref, w_ref, ma_ref, mb_ref, mc_ref, bias_ref, o_ref):
    b = (zb_ref[...] * _silu(bg_ref[...])).astype(BF16)
    bias = bias_ref[...]
    y = None
    for n, (br, m_ref) in enumerate(((a_ref[...], ma_ref), (b, mb_ref), (c_ref[...], mc_ref))):
        pb = jnp.dot(br, w_ref[n], preferred_element_type=F32)
        gate = 1.0 / (1.0 + jnp.exp(-(m_ref[...] + bias[n:n + 1])))
        y = gate * pb if y is None else y + gate * pb
    o_ref[...] = y.astype(o_ref.dtype)


def _merge(a_out, zb, p1, c_out, p2, w_proj_bf, merge_b, layer, tm=512, tn=512):
    S = a_out.shape[0]
    W, D = BRANCH_W, D_MODEL
    bgb = (P1_COLS - W) // W
    mcb = W // tn
    branch = pl.BlockSpec((tm, W), lambda i, j: (i, 0))
    mspec = lambda n: pl.BlockSpec((tm, tn), lambda i, j: (i, mcb + n * (D // tn) + j))
    return pl.pallas_call(
        _merge_kernel,
        grid=(S // tm, D // tn),
        in_specs=[branch, pl.BlockSpec((None, tm, W), lambda i, j: (0, i, 0)),
                  pl.BlockSpec((tm, W), lambda i, j: (i, bgb)), branch,
                  pl.BlockSpec((None, N_BRANCH, W, tn), lambda i, j: (layer, 0, 0, j)),
                  mspec(0), mspec(1), mspec(2),
                  pl.BlockSpec((None, N_BRANCH, tn), lambda i, j: (layer, 0, j))],
        out_specs=pl.BlockSpec((tm, tn), lambda i, j: (i, j)),
        out_shape=jax.ShapeDtypeStruct((S, D), BF16),
        compiler_params=_params("parallel", "arbitrary"),
        name="merge",
    )(a_out, zb, p1, c_out, w_proj_bf, p2, p2, p2, merge_b)


def _out_kernel(x_ref, y_ref, w_ref, g_ref, xo_ref, ho_ref):
    x = x_ref[...] + jnp.dot(y_ref[...], w_ref[...], preferred_element_type=F32)
    xo_ref[...] = x
    ms = jnp.mean(x * x, axis=-1, keepdims=True)
    ho_ref[...] = (x * lax.rsqrt(ms + NORM_EPS) * g_ref[...]).astype(ho_ref.dtype)


def _out_proj(x, y, w_out_bf, layer, g, h_dtype, tm=512):
    S, D = x.shape
    row = pl.BlockSpec((tm, D), lambda i: (i, 0))
    return pl.pallas_call(
        _out_kernel,
        grid=(S // tm,),
        in_specs=[row, row, pl.BlockSpec((None, D, D), lambda i: (layer, 0, 0)), pl.BlockSpec((1, D), lambda i: (0, 0))],
        out_specs=(row, row),
        out_shape=(jax.ShapeDtypeStruct((S, D), F32), jax.ShapeDtypeStruct((S, D), h_dtype)),
        compiler_params=_params("parallel"),
        name="out_proj",
    )(x, y, w_out_bf, g.reshape(1, D))


def _encode(xb, norm_g, final_g, w_in, merge_b, rel_bias, hy_conv, hy_w1, hy_b1, hy_freq, hy_w2, hy_b2, hy_w3,
            hy_skip, diff_lam, diff_g, w_proj_bf, w_out_bf, tb):
    S = xb.shape[0]
    depth = w_in.shape[0]
    N1, N2 = tb["N1"], tb["N2"]
    h = _rmsnorm(xb, norm_g[0], BF16)
    for l in range(depth):
        qkv = [_in_proj(h, w_in, l, COL_A + g * A_GROUP_COLS, A_GROUP_COLS, dil=DIL_PAIRS[g][1]) for g in range(3)]
        p1 = _in_proj(h, w_in, l, COL_P1, P1_COLS)
        cqkv = _in_proj(h, w_in, l, COL_C, C_QKV_COLS, dil=1, scale0=C_Q_SCALE)
        p2 = _in_proj(h, w_in, l, COL_P2, P2_COLS)

        others = [_dilated_group(rel_bias, qkv[g], g, None, None) for g in (1, 2)]
        a_out = _dilated_group(rel_bias, qkv[0], 0, others, p1)

        u = _short_conv(p1, hy_conv, l, BRANCH_W)
        filt = _hyena_filters(S, hy_w1, hy_b1, hy_freq, hy_w2, hy_b2, hy_w3, hy_skip, l)
        fa_out = _outer_dft(tb["fa"], filt.reshape(HY_ORDER, N1, N2, BRANCH_W), None)
        hspec = _filter_spectrum(tb["gf"], fa_out.reshape(HY_ORDER, N1, 2 * N2, BRANCH_W))
        z = _long_conv(u, 0, u, 1, hspec, 0, tb)
        zb = _long_conv(z, 0, u, 2, hspec, 1, tb)

        c_out = _diff_attention(rel_bias, cqkv, diff_lam, diff_g, l, p2, 0)

        y = _merge(a_out, zb, p1, c_out, p2, w_proj_bf, merge_b, l)
        if l + 1 < depth:
            xb, h = _out_proj(xb, y, w_out_bf, l, norm_g[l + 1], BF16)
        else:
            _, h = _out_proj(xb, y, w_out_bf, l, final_g, F32)
    return h


def kernel(x, norm_g, final_g, w_in, merge_b, rel_bias, hy_conv, hy_w1, hy_b1, hy_freq, hy_w2, hy_b2, hy_w3,
           hy_skip, diff_lam, diff_g, w_proj, w_out):
    B, S, D = x.shape
    w_proj_bf = w_proj.astype(BF16)
    w_out_bf = w_out.astype(BF16)
    tb = _dft_tables(S)
    outs = [_encode(x[b], norm_g, final_g, w_in, merge_b, rel_bias, hy_conv, hy_w1, hy_b1, hy_freq, hy_w2, hy_b2,
                    hy_w3, hy_skip, diff_lam, diff_g, w_proj_bf, w_out_bf, tb) for b in range(B)]
    return outs[0].reshape(1, S, D) if B == 1 else jnp.stack(outs)
```

```python
import functools
import math

import jax
import jax.numpy as jnp
from jax import lax
from jax.experimental import pallas as pl
from jax.experimental.pallas import tpu as pltpu

F32 = jnp.float32
BF16 = jnp.bfloat16

LANES = 128
SUBLANES = 8
D_MODEL = 2048
BRANCH_W = 1024
N_BRANCH = 3
DIL_PAIRS = ((128, 1), (512, 4), (2048, 16))
A_HEADS = 8
A_HEAD_DIM = 128
REACH = 64
A_SUB = 128
A_KW = A_SUB + 2 * REACH
C_HEADS = 8
C_QK_DIM = 64
C_V_DIM = 128
HY_BANDS = 16
HY_FFN = 64
HY_ORDER = 2
HY_TARGET = 1e-2
HY_FAST_PCT = 0.3
HY_SLOW_PCT = 1.5
N_BUCKETS = 32
REL_MAX_DIST = 1024
NORM_EPS = 1e-6
NEG = -1e30

A_GROUP_COLS = 3 * A_HEADS * A_HEAD_DIM
A_QKV_COLS = 3 * A_GROUP_COLS
P1_COLS = BRANCH_W + 3 * BRANCH_W + BRANCH_W
C_QKV_COLS = 3 * C_HEADS * C_V_DIM
P2_COLS = BRANCH_W + N_BRANCH * D_MODEL
COL_A, COL_P1, COL_C, COL_P2 = 0, A_QKV_COLS, A_QKV_COLS + P1_COLS, A_QKV_COLS + P1_COLS + C_QKV_COLS

VMEM_LIMIT = 56 * 1024 * 1024
HIGHEST = lax.Precision.HIGHEST
LOG2E = math.log2(math.e)
C_Q_SCALE = LOG2E / math.sqrt(C_QK_DIM)


def _bucket_thresholds():
    nb = N_BUCKETS // 2
    max_exact = nb // 2
    thr = []
    for k in range(1, nb - max_exact):
        n = max_exact
        while int(math.log(n / max_exact) / math.log(REL_MAX_DIST / max_exact) * (nb - max_exact)) < k:
            n += 1
        thr.append(n)
    return tuple(thr)


BUCKET_THR = _bucket_thresholds()
FAR_DIST = BUCKET_THR[-1]


def _params(*sem):
    return pltpu.CompilerParams(dimension_semantics=sem, vmem_limit_bytes=VMEM_LIMIT)


def _rel_bias(rel, tab_ref, col):
    nb = N_BUCKETS // 2
    n = jnp.abs(rel)
    large = jnp.full(rel.shape, nb // 2, jnp.int32)
    for thr in BUCKET_THR:
        large = large + (n >= thr).astype(jnp.int32)
    bucket = jnp.where(n < nb // 2, n, large) + jnp.where(rel > 0, nb, 0)
    out = jnp.full(rel.shape, tab_ref[0, col], F32)
    for j in range(1, N_BUCKETS):
        out = jnp.where(bucket == j, tab_ref[j, col], out)
    return out


def _silu(x):
    return x * (1.0 / (1.0 + jnp.exp(-x)))


def _split_bf16(x):
    hi = x.astype(BF16)
    return hi, (x - hi.astype(F32)).astype(BF16)


def _const3(a):
    hi, lo = _split_bf16(a)
    return jnp.concatenate([hi, hi, lo], axis=-1)


def _dot3(a3, x):
    hi, lo = _split_bf16(x)
    return jnp.dot(a3, jnp.concatenate([hi, lo, hi], axis=0), preferred_element_type=F32)


def _rmsnorm_kernel(x_ref, g_ref, o_ref):
    x = x_ref[...]
    ms = jnp.mean(x * x, axis=-1, keepdims=True)
    o_ref[...] = (x * lax.rsqrt(ms + NORM_EPS) * g_ref[...]).astype(o_ref.dtype)


def _rmsnorm(x, g, out_dtype, tm=512):
    S, D = x.shape
    return pl.pallas_call(
        _rmsnorm_kernel,
        grid=(S // tm,),
        in_specs=[pl.BlockSpec((tm, D), lambda i: (i, 0)), pl.BlockSpec((1, D), lambda i: (0, 0))],
        out_specs=pl.BlockSpec((tm, D), lambda i: (i, 0)),
        out_shape=jax.ShapeDtypeStruct((S, D), out_dtype),
        compiler_params=_params("parallel"),
        name="rmsnorm",
    )(x, g.reshape(1, D))


def _proj_kernel(h_ref, w_ref, o_ref, wb_ref, *acc_ref, dil, scale0):
    @pl.when(pl.program_id(1) == 0)
    def _():
        wb_ref[...] = w_ref[...].astype(BF16)

    acc = jnp.dot(h_ref[...], wb_ref[...], preferred_element_type=F32)
    if scale0 is not None:
        acc = acc * jnp.where(pl.program_id(0) == 0, scale0, 1.0)
    tm, tn = acc.shape
    if dil is None:
        o_ref[...] = acc
    elif dil == 1:
        for c in range(tn // LANES):
            o_ref[c, 0] = acc[:, c * LANES:(c + 1) * LANES].astype(o_ref.dtype)
    else:
        for c in range(tn // LANES):
            acc_ref[0][c] = acc[:, c * LANES:(c + 1) * LANES]
            for r in range(dil):
                o_ref[c, r] = acc_ref[0][c, pl.ds(r, tm // dil, stride=dil), :].astype(o_ref.dtype)


def _in_proj(h, w_in, layer, col0, ncols, dil=None, scale0=None, tm=1024, tn=1024):
    S, D = h.shape
    assert col0 % tn == 0 and ncols % tn == 0
    jb = col0 // tn
    scratch = [pltpu.VMEM((D, tn), BF16)]
    if dil is None:
        out_shape = jax.ShapeDtypeStruct((S, ncols), F32)
        out_spec = pl.BlockSpec((tm, tn), lambda j, i: (i, j))
    else:
        out_shape = jax.ShapeDtypeStruct((ncols // LANES, dil, S // dil, LANES), BF16)
        out_spec = pl.BlockSpec((tn // LANES, dil, tm // dil, LANES), lambda j, i: (j, 0, i, 0))
        if dil > 1:
            scratch.append(pltpu.VMEM((tn // LANES, tm, LANES), F32))
    return pl.pallas_call(
        functools.partial(_proj_kernel, dil=dil, scale0=scale0),
        grid=(ncols // tn, S // tm),
        in_specs=[pl.BlockSpec((tm, D), lambda j, i: (i, 0)),
                  pl.BlockSpec((None, D, tn), lambda j, i: (layer, 0, jb + j))],
        out_specs=out_spec,
        out_shape=out_shape,
        scratch_shapes=scratch,
        compiler_params=_params("parallel", "arbitrary"),
        name="in_proj",
    )(h, w_in)


def _dil_kernel(*refs, dil, col0, others, n, tq):
    it = iter(refs)
    tab_ref, q_ref, k_ref, v_ref = next(it), next(it), next(it), next(it)
    other_refs = [(next(it), next(it)) for _ in others]
    gate_ref = next(it) if others else None
    o_ref = next(it)
    lse_ref = None if others else next(it)
    bm_ref = next(it)
    pos_refs = [(next(it), next(it)) for _ in others]

    h, r, i = pl.program_id(0), pl.program_id(1), pl.program_id(2)
    deltas = (-REACH, 0, -2 * REACH)

    @pl.when((r == 0) & (i == 0))
    def _():
        t = lax.broadcasted_iota(jnp.int32, (A_SUB, A_KW), 0)
        u = lax.broadcasted_iota(jnp.int32, (A_SUB, A_KW), 1)
        for idx, delta in enumerate(deltas):
            rel = u + delta - t
            bias = _rel_bias(rel * dil, tab_ref, col0 + h)
            bm_ref[idx] = jnp.where(jnp.abs(rel) <= REACH, bias, NEG)

    for (src_o, src_l), (dst_o, dst_l), d in zip(other_refs, pos_refs, others):
        for rr in range(d):
            dst_o[pl.ds(rr, tq // d, stride=d), :] = src_o[rr]
            dst_l[pl.ds(rr, tq // d, stride=d), :] = src_l[rr]

    for sb in range(tq // A_SUB):
        rows = slice(sb * A_SUB, (sb + 1) * A_SUB)
        m0 = i * tq + sb * A_SUB
        ks = pl.multiple_of(jnp.clip(m0 - REACH, 0, n - A_KW), REACH)
        sel = jnp.where(m0 == 0, 1, jnp.where(m0 == n - A_SUB, 2, 0))
        k = k_ref[pl.ds(ks, A_KW), :]
        v = v_ref[pl.ds(ks, A_KW), :]
        s = lax.dot_general(q_ref[rows, :], k, (((1,), (1,)), ((), ())), preferred_element_type=F32)
        s = s * (1.0 / math.sqrt(A_HEAD_DIM)) + bm_ref[sel]
        m = jnp.max(s, axis=-1, keepdims=True)
        p = jnp.exp(s - m)
        l = jnp.sum(p, axis=-1, keepdims=True)
        o = jnp.dot(p.astype(BF16), v, preferred_element_type=F32) / l
        lse = jnp.broadcast_to(m + jnp.log(l), o.shape)
        if others:
            lses = [lse] + [dst_l[rows, :] for _, dst_l in pos_refs]
            outs = [o] + [dst_o[rows, :] for dst_o, _ in pos_refs]
            mx = functools.reduce(jnp.maximum, lses)
            ws = [jnp.exp(x - mx) for x in lses]
            o = sum(w * x for w, x in zip(ws, outs)) / sum(ws)
            o_ref[rows, :] = (o * _silu(gate_ref[rows, :])).astype(o_ref.dtype)
        else:
            o_ref[rows, :] = o
            lse_ref[rows, :] = lse


def _dilated_group(tab, qkv, g, others, gate_src, tq=512):
    win, dil = DIL_PAIRS[g]
    assert win // (2 * dil) == REACH
    _, _, n, hd = qkv.shape
    tq = min(tq, n)
    assert qkv.shape[1] == dil and n % tq == 0 and n >= A_KW
    last = others is not None
    qblk = pl.BlockSpec((None, None, tq, hd), lambda h, r, i: (h, r, i, 0))
    in_specs = [pl.BlockSpec(memory_space=pltpu.SMEM), qblk,
                pl.BlockSpec((None, None, n, hd), lambda h, r, i: (A_HEADS + h, r, 0, 0)),
                pl.BlockSpec((None, None, n, hd), lambda h, r, i: (2 * A_HEADS + h, r, 0, 0))]
    args = [tab, qkv, qkv, qkv]
    scratch = [pltpu.VMEM((3, A_SUB, A_KW), F32)]
    other_dils = ()
    if last:
        assert dil == 1
        other_dils = tuple(o.shape[1] for o, _ in others)
        for (o, lse), d in zip(others, other_dils):
            spec = pl.BlockSpec((None, d, tq // d, hd), lambda h, r, i: (h, 0, i, 0))
            in_specs += [spec, spec]
            args += [o, lse]
            scratch += [pltpu.VMEM((tq, hd), F32), pltpu.VMEM((tq, hd), F32)]
        in_specs.append(pl.BlockSpec((tq, hd), lambda h, r, i: (i, h)))
        args.append(gate_src)
        out_shape = jax.ShapeDtypeStruct((n, A_HEADS * hd), BF16)
        out_specs = pl.BlockSpec((tq, hd), lambda h, r, i: (i, h))
    else:
        out_shape = (jax.ShapeDtypeStruct((A_HEADS, dil, n, hd), F32),) * 2
        out_specs = (qblk, qblk)
    return pl.pallas_call(
        functools.partial(_dil_kernel, dil=dil, col0=g * A_HEADS, others=other_dils, n=n, tq=tq),
        grid=(A_HEADS, dil, n // tq),
        in_specs=in_specs,
        out_specs=out_specs,
        out_shape=out_shape,
        scratch_shapes=scratch,
        compiler_params=_params("arbitrary", "arbitrary", "arbitrary"),
        name=f"dilated_attn_g{g}",
    )(*args)


def _diff_kernel(tab_ref, dl_ref, g_ref, q_ref, k_ref, v_ref, gate_ref, o_ref, strip_ref, v2_ref, *,
                 lam_init, S, tq, tk, nsub, dmin, dmax):
    h, i = pl.program_id(0), pl.program_id(1)
    width = strip_ref.shape[1]
    nk = S // tk
    tks = tk // nsub
    col = 3 * A_HEADS + h
    far_lo = tab_ref[N_BUCKETS // 2 - 1, col] * LOG2E
    far_hi = tab_ref[N_BUCKETS - 1, col] * LOG2E

    @pl.when(i == 0)
    def _():
        t = lax.broadcasted_iota(jnp.int32, (tq, width), 0)
        c = lax.broadcasted_iota(jnp.int32, (tq, width), 1)
        strip_ref[...] = _rel_bias(c - t + dmin, tab_ref, col) * LOG2E
        v2_ref[:, :C_V_DIM] = v_ref[...]
        v2_ref[:, C_V_DIM:] = jnp.ones((S, C_V_DIM), BF16)

    q = q_ref[...]
    lane = lax.broadcasted_iota(jnp.int32, q.shape, 1)
    zero = jnp.zeros_like(q)
    qs = (jnp.where(lane < C_QK_DIM, q, zero), jnp.where(lane >= C_QK_DIM, q, zero))

    def tile(j, carry, near):
        k0 = pl.multiple_of(j * tk, tk)
        if near:
            b0 = pl.multiple_of(j * tk - i * tq - dmin, LANES)
            shift = 0.0
        else:
            shift = jnp.where(j * tk < i * tq, far_lo, far_hi)
        ss = []
        for u in range(nsub):
            kt = k_ref[pl.ds(k0 + u * tks, tks), :]
            su = [lax.dot_general(qc, kt, (((1,), (1,)), ((), ())), preferred_element_type=F32) for qc in qs]
            if near:
                b = strip_ref[:, pl.ds(b0 + u * tks, tks)]
                su = [s + b for s in su]
            ss.append(su)
        for u in range(nsub):
            vt = v2_ref[pl.ds(k0 + u * tks, tks), :]
            ms = [jnp.maximum(m, jnp.max(s, axis=-1, keepdims=True) + shift) for s, (m, _) in zip(ss[u], carry)]
            ps = [jnp.exp2((s - (m_new - shift)).astype(BF16)) for s, m_new in zip(ss[u], ms)]
            pvs = [jnp.dot(p, vt, preferred_element_type=F32) for p in ps]
            carry = tuple((m_new, jnp.exp2(m - m_new) * acc + pv) for m_new, pv, (m, acc) in zip(ms, pvs, carry))
        return carry

    big = nk + 2
    j_lo = jnp.clip((dmin + i * tq + big * tk) // tk + 1 - big, 0, nk)
    j_hi = jnp.clip((dmax + i * tq + tk - 1) // tk, j_lo, nk)
    carry = ((jnp.full((tq, 1), NEG, F32), jnp.zeros((tq, 2 * C_V_DIM), F32)),) * 2
    carry = lax.fori_loop(0, j_lo, functools.partial(tile, near=False), carry)
    carry = lax.fori_loop(j_lo, j_hi, functools.partial(tile, near=True), carry)
    carry = lax.fori_loop(j_hi, nk, functools.partial(tile, near=False), carry)
    on = [acc[:, :C_V_DIM] / acc[:, C_V_DIM:] for _, acc in carry]
    dl = dl_ref[...]
    lam = (jnp.exp(jnp.sum(dl[0:1] * dl[1:2], axis=-1, keepdims=True))
           - jnp.exp(jnp.sum(dl[2:3] * dl[3:4], axis=-1, keepdims=True)) + lam_init)
    o = on[0] - lam * on[1]
    y = o * lax.rsqrt(jnp.mean(o * o, axis=-1, keepdims=True) + NORM_EPS) * g_ref[...]
    o_ref[...] = (y * (1.0 - lam_init) * _silu(gate_ref[...])).astype(o_ref.dtype)


def _diff_attention(tab, cqkv, diff_lam, diff_g, layer, gate_src, gate_col, tq=512, tk=2048, nsub=4):
    _, _, S, hd = cqkv.shape
    assert S % tk == 0 and S % tq == 0
    lam_init = 0.8 - 0.6 * math.exp(-0.3 * layer)
    dmax = pl.cdiv(FAR_DIST + tq - 1, LANES) * LANES
    dmin = -pl.cdiv(FAR_DIST + tk - 1, LANES) * LANES
    width = dmax - dmin + tk
    gcb = gate_col // hd
    return pl.pallas_call(
        functools.partial(_diff_kernel, lam_init=lam_init, S=S, tq=tq, tk=tk, nsub=nsub, dmin=dmin, dmax=dmax),
        grid=(C_HEADS, S // tq),
        in_specs=[pl.BlockSpec(memory_space=pltpu.SMEM),
                  pl.BlockSpec((None, 4, C_QK_DIM), lambda h, i: (layer, 0, 0)),
                  pl.BlockSpec((None, 1, C_V_DIM), lambda h, i: (layer, 0, 0)),
                  pl.BlockSpec((None, None, tq, hd), lambda h, i: (h, 0, i, 0)),
                  pl.BlockSpec((None, None, S, hd), lambda h, i: (C_HEADS + h, 0, 0, 0)),
                  pl.BlockSpec((None, None, S, hd), lambda h, i: (2 * C_HEADS + h, 0, 0, 0)),
                  pl.BlockSpec((tq, hd), lambda h, i: (i, gcb + h))],
        out_specs=pl.BlockSpec((tq, hd), lambda h, i: (i, h)),
        out_shape=jax.ShapeDtypeStruct((S, C_HEADS * hd), BF16),
        scratch_shapes=[pltpu.VMEM((tq, width), F32), pltpu.VMEM((S, 2 * C_V_DIM), BF16)],
        compiler_params=_params("arbitrary", "arbitrary"),
        name="diff_attn",
    )(tab, diff_lam, diff_g.reshape(diff_g.shape[0], 1, C_V_DIM), cqkv, cqkv, cqkv, gate_src)


def _short_conv_kernel(b_ref, w_ref, o_ref):
    b = b_ref[...]
    S = b.shape[0]
    row = lax.broadcasted_iota(jnp.int32, b.shape, 0)
    prev = jnp.where(row == 0, 0.0, pltpu.roll(b, 1, 0))
    nxt = jnp.where(row == S - 1, 0.0, pltpu.roll(b, S - 1, 0))
    w = w_ref[...]
    o_ref[...] = w[0:1] * prev + w[1:2] * b + w[2:3] * nxt


def _short_conv(p1, hy_conv, layer, col0):
    S = p1.shape[0]
    nct = BRANCH_W // LANES
    cb = col0 // LANES
    return pl.pallas_call(
        _short_conv_kernel,
        grid=(3 * nct,),
        in_specs=[pl.BlockSpec((S, LANES), lambda c: (0, cb + c)),
                  pl.BlockSpec((None, 3, LANES), lambda c: (layer, 0, c))],
        out_specs=pl.BlockSpec((None, S, LANES), lambda c: (c // nct, 0, c % nct)),
        out_shape=jax.ShapeDtypeStruct((3, S, BRANCH_W), F32),
        compiler_params=_params("parallel"),
        name="short_conv",
    )(p1, hy_conv)


def _filter_kernel(w1t_ref, w1c_ref, w1s_ref, b1_ref, fr_ref, w2_ref, b2_ref, w3a_ref, w3b_ref, w3a0_ref, w3b0_ref,
                   skip_ref, ad_ref, o_ref, *, L, tr):
    j = pl.program_id(0)
    N = 2 * L

    def hidden(pos):
        t = pos * (1.0 / (L - 1))
        w = pos * (2.0 * math.pi / L)
        fb = (1e-4 + lax.broadcasted_iota(jnp.int32, (HY_BANDS, 1), 0).astype(F32)
              * ((HY_BANDS - 1 - 1e-4) / (HY_BANDS - 1)))
        ang = fb * w
        pre = (w1t_ref[...] * t
               + jnp.dot(w1c_ref[...], jnp.cos(ang), precision=HIGHEST, preferred_element_type=F32)
               - jnp.dot(w1s_ref[...], jnp.sin(ang), precision=HIGHEST, preferred_element_type=F32)
               + b1_ref[...])
        fr = fr_ref[...]
        hid = jnp.sin(fr[:, 0:1] * pre)
        hid = jnp.sin(fr[:, 1:2] * (jnp.dot(w2_ref[...], hid, precision=HIGHEST, preferred_element_type=F32)
                                    + b2_ref[...]))
        return hid.T

    def position(shape, dim):
        row = j * tr + lax.broadcasted_iota(jnp.int32, shape, dim)
        return row, jnp.where(row < L, row, N - row).astype(F32)

    _, pos_lane = position((1, tr), 1)
    row, pos_sub = position((tr, 1), 0)
    hid = hidden(pos_lane)
    decay = jnp.exp(-(pos_sub * (1.0 / (L - 1))) * ad_ref[...])
    for o, w3_ref in enumerate((w3a_ref, w3b_ref)):
        g = jnp.dot(hid, w3_ref[...], precision=HIGHEST, preferred_element_type=F32) * decay
        o_ref[o] = jnp.where(row == L, 0.0, g)

    @pl.when(j == 0)
    def _():
        hid0 = hidden(jnp.zeros((1, LANES), F32))[0:SUBLANES]
        r8 = lax.broadcasted_iota(jnp.int32, (SUBLANES, 1), 0)
        for o, w3_ref in enumerate((w3a0_ref, w3b0_ref)):
            hb0 = jnp.dot(hid0, w3_ref[...], precision=HIGHEST, preferred_element_type=F32)
            o_ref[o, 0:SUBLANES, :] = o_ref[o, 0:SUBLANES, :] + jnp.where(r8 == 0, hb0 + skip_ref[o], 0.0)


def _hyena_filters(L, hy_w1, hy_b1, hy_freq, hy_w2, hy_b2, hy_w3, hy_skip, layer, tr=512):
    assert HY_ORDER == 2
    N = 2 * L
    W = BRANCH_W
    nb = HY_BANDS
    w1 = hy_w1[layer].T
    max_decay = math.log(HY_TARGET) / HY_FAST_PCT
    min_decay = math.log(HY_TARGET) / HY_SLOW_PCT
    absdelta = jnp.abs(jnp.linspace(min_decay, max_decay, W, dtype=F32)).reshape(1, W)
    full = lambda shape: pl.BlockSpec(shape, lambda j: (0,) * len(shape))
    nfwd = L // tr
    w3_spec = lambda o, bwd: pl.BlockSpec(
        (None, HY_FFN, W), lambda j: (layer, 0, 2 * o + ((j >= nfwd).astype(jnp.int32) if bwd is None else bwd)))
    return pl.pallas_call(
        functools.partial(_filter_kernel, L=L, tr=tr),
        grid=(N // tr,),
        in_specs=[full((HY_FFN, 1)), full((HY_FFN, nb)), full((HY_FFN, nb)), full((HY_FFN, 1)), full((HY_FFN, 2)),
                  full((HY_FFN, HY_FFN)), full((HY_FFN, 1)),
                  w3_spec(0, None), w3_spec(1, None), w3_spec(0, 1), w3_spec(1, 1),
                  pl.BlockSpec((None, HY_ORDER, 1, W), lambda j: (layer, 0, 0, 0)),
                  full((1, W))],
        out_specs=pl.BlockSpec((HY_ORDER, tr, W), lambda j: (0, j, 0)),
        out_shape=jax.ShapeDtypeStruct((HY_ORDER, N, W), F32),
        compiler_params=_params("arbitrary"),
        name="hyena_filter",
    )(w1[:, 0:1], w1[:, 1:1 + nb], w1[:, 1 + nb:], hy_b1[layer].reshape(HY_FFN, 1), hy_freq[layer].T,
      hy_w2[layer].T, hy_b2[layer].reshape(HY_FFN, 1), hy_w3, hy_w3, hy_w3, hy_w3,
      hy_skip.reshape(hy_skip.shape[0], HY_ORDER, 1, W), absdelta)


def _outer_dft_kernel(f_ref, x_ref, *rest, gated):
    gate_ref = rest[0] if gated else None
    o_ref, xs_ref, ys_ref = rest[-3:]
    K, sub, tw = x_ref.shape
    M = o_ref.shape[0]
    nch = tw // LANES
    for c in range(nch):
        xs_ref[c] = x_ref[:, :, c * LANES:(c + 1) * LANES].reshape(K * sub, LANES)
    f3 = f_ref[...]
    for s in range(sub):
        x = jnp.concatenate([xs_ref[c, pl.ds(s, K, stride=sub), :] for c in range(nch)], axis=1)
        y = _dot3(f3, x)
        for c in range(nch):
            ys_ref[c, pl.ds(s, M, stride=sub), :] = y[:, c * LANES:(c + 1) * LANES]
    for c in range(nch):
        y = ys_ref[c].reshape(M, sub, LANES)
        if gated:
            y = y * gate_ref[:, :, c * LANES:(c + 1) * LANES]
        o_ref[:, :, c * LANES:(c + 1) * LANES] = y


def _outer_dft(f3, x, xb, gate=None, gb=0, tw=BRANCH_W):
    _, K, N2, W = x.shape
    M = f3.shape[0]
    B = x.shape[0] if xb is None else 1
    x_idx = (lambda b: b) if xb is None else (lambda b: xb)
    blk = lambda rows, idx: pl.BlockSpec((None, rows, SUBLANES, tw), lambda b, t, c: (idx(b), 0, t, c))
    in_specs = [pl.BlockSpec((M, 3 * K), lambda b, t, c: (0, 0)), blk(K, x_idx)]
    args = [f3, x]
    if gate is not None:
        in_specs.append(blk(M, lambda b: gb))
        args.append(gate)
    return pl.pallas_call(
        functools.partial(_outer_dft_kernel, gated=gate is not None),
        grid=(B, N2 // SUBLANES, W // tw),
        in_specs=in_specs,
        out_specs=blk(M, lambda b: b),
        out_shape=jax.ShapeDtypeStruct((B, M, N2, W), F32),
        scratch_shapes=[pltpu.VMEM((tw // LANES, K * SUBLANES, LANES), F32),
                        pltpu.VMEM((tw // LANES, M * SUBLANES, LANES), F32)],
        compiler_params=_params("parallel", "parallel", "parallel"),
        name="dft_outer",
    )(*args)


def _spectrum_kernel(gf_ref, a_ref, o_ref):
    o_ref[...] = _dot3(gf_ref[...], a_ref[...])


def _filter_spectrum(gf3, a):
    O, N1, R, W = a.shape
    return pl.pallas_call(
        _spectrum_kernel,
        grid=(O, N1),
        in_specs=[pl.BlockSpec((None, R, 3 * R), lambda o, k: (k, 0, 0)),
                  pl.BlockSpec((None, None, R, W), lambda o, k: (o, k, 0, 0))],
        out_specs=pl.BlockSpec((None, None, R, W), lambda o, k: (o, k, 0, 0)),
        out_shape=jax.ShapeDtypeStruct(a.shape, F32),
        compiler_params=_params("parallel", "parallel"),
        name="filter_spectrum",
    )(gf3, a)


def _conv_mid_kernel(gf_ref, gi_ref, a_ref, h_ref, o_ref):
    half = a_ref.shape[0] // 2
    x = _dot3(gf_ref[...], a_ref[...])
    xr, xi = x[:half], x[half:]
    hr, hi = h_ref[:half, :], h_ref[half:, :]
    y = jnp.concatenate([xr * hr - xi * hi, xr * hi + xi * hr], axis=0)
    o_ref[...] = _dot3(gi_ref[...], y)


def _conv_mid(gf3, gi3, a, hspec, order):
    N1, R, W = a.shape
    return pl.pallas_call(
        _conv_mid_kernel,
        grid=(N1,),
        in_specs=[pl.BlockSpec((None, R, 3 * R), lambda k: (k, 0, 0)),
                  pl.BlockSpec((None, R, 3 * R), lambda k: (k, 0, 0)),
                  pl.BlockSpec((None, R, W), lambda k: (k, 0, 0)),
                  pl.BlockSpec((None, None, R, W), lambda k: (order, k, 0, 0))],
        out_specs=pl.BlockSpec((None, R, W), lambda k: (k, 0, 0)),
        out_shape=jax.ShapeDtypeStruct(a.shape, F32),
        compiler_params=_params("parallel"),
        name="conv_mid",
    )(gf3, gi3, a, hspec)


def _dft_tables(L):
    N = 2 * L
    N2 = LANES
    N1 = N // N2
    NS = pl.cdiv(N1 // 2 + 1, SUBLANES) * SUBLANES
    two_pi = 2.0 * math.pi
    ks = jnp.arange(NS, dtype=jnp.int32)
    t1 = jnp.arange(N1, dtype=jnp.int32)
    live = (ks <= N1 // 2).astype(F32)
    ang = ((ks[:, None] * t1[None, :]) % N1).astype(F32) * (two_pi / N1)
    cs = jnp.stack([jnp.cos(ang), -jnp.sin(ang)], axis=1) * live[:, None, None]
    fa = cs.reshape(2 * NS, N1)
    weight = jnp.where((ks == 0) | (ks == N1 // 2), 1.0, 2.0) / N
    fi = jnp.transpose(cs * weight[:, None, None], (2, 0, 1)).reshape(N1, 2 * NS)[:N1 // 2]
    t2 = jnp.arange(N2, dtype=jnp.int32)
    k = ks[:, None, None] + N1 * t2[None, :, None]
    ph = ((k * t2[None, None, :]) % N).astype(F32) * (two_pi / N)
    gr, gim = jnp.cos(ph), -jnp.sin(ph)
    gf = jnp.concatenate([jnp.concatenate([gr, -gim], axis=2), jnp.concatenate([gim, gr], axis=2)], axis=1)
    gi = jnp.swapaxes(gf, 1, 2)
    return {"fa": _const3(fa), "fh": _const3(fa[:, :N1 // 2]), "fi": _const3(fi), "gf": _const3(gf),
            "gi": _const3(gi), "N1": N1, "N2": N2, "NS": NS}


def _long_conv(z, zi, gate, gi_, hspec, order, tb):
    _, S, W = z.shape
    N1, N2, NS = tb["N1"], tb["N2"], tb["NS"]
    split = lambda t: t.reshape(t.shape[0], N1 // 2, N2, W)
    a = _outer_dft(tb["fh"], split(z), zi)
    b = _conv_mid(tb["gf"], tb["gi"], a.reshape(NS, 2 * N2, W), hspec, order)
    y = _outer_dft(tb["fi"], b.reshape(1, 2 * NS, N2, W), 0, split(gate), gi_)
    return y.reshape(1, S, W)


def _merge_kernel(a_ref, zb_ref, bg_ref, c_ref, w_ref, ma_ref, mb_ref, mc_ref, bias_ref, o_ref):
    b = (zb_ref[...] * _silu(bg_ref[...])).astype(BF16)
    bias = bias_ref[...]
    y = None
    for n, (br, m_ref) in enumerate(((a_ref[...], ma_ref), (b, mb_ref), (c_ref[...], mc_ref))):
        pb = jnp.dot(br, w_ref[n], preferred_element_type=F32)
        gate = 1.0 / (1.0 + jnp.exp(-(m_ref[...] + bias[n:n + 1])))
        y = gate * pb if y is None else y + gate * pb
    o_ref[...] = y.astype(o_ref.dtype)


def _merge(a_out, zb, p1, c_out, p2, w_proj_bf, merge_b, layer, tm=512, tn=512):
    S = a_out.shape[0]
    W, D = BRANCH_W, D_MODEL
    bgb = (P1_COLS - W) // W
    mcb = W // tn
    branch = pl.BlockSpec((tm, W), lambda i, j: (i, 0))
    mspec = lambda n: pl.BlockSpec((tm, tn), lambda i, j: (i, mcb + n * (D // tn) + j))
    return pl.pallas_call(
        _merge_kernel,
        grid=(S // tm, D // tn),
        in_specs=[branch, pl.BlockSpec((None, tm, W), lambda i, j: (0, i, 0)),
                  pl.BlockSpec((tm, W), lambda i, j: (i, bgb)), branch,
                  pl.BlockSpec((None, N_BRANCH, W, tn), lambda i, j: (layer, 0, 0, j)),
                  mspec(0), mspec(1), mspec(2),
                  pl.BlockSpec((None, N_BRANCH, tn), lambda i, j: (layer, 0, j))],
        out_specs=pl.BlockSpec((tm, tn), lambda i, j: (i, j)),
        out_shape=jax.ShapeDtypeStruct((S, D), BF16),
        compiler_params=_params("parallel", "arbitrary"),
        name="merge",
    )(a_out, zb, p1, c_out, w_proj_bf, p2, p2, p2, merge_b)


def _out_kernel(x_ref, y_ref, w_ref, g_ref, xo_ref, ho_ref):
    x = x_ref[...] + jnp.dot(y_ref[...], w_ref[...], preferred_element_type=F32)
    xo_ref[...] = x
    ms = jnp.mean(x * x, axis=-1, keepdims=True)
    ho_ref[...] = (x * lax.rsqrt(ms + NORM_EPS) * g_ref[...]).astype(ho_ref.dtype)


def _out_proj(x, y, w_out_bf, layer, g, h_dtype, tm=512):
    S, D = x.shape
    row = pl.BlockSpec((tm, D), lambda i: (i, 0))
    return pl.pallas_call(
        _out_kernel,
        grid=(S // tm,),
        in_specs=[row, row, pl.BlockSpec((None, D, D), lambda i: (layer, 0, 0)), pl.BlockSpec((1, D), lambda i: (0, 0))],
        out_specs=(row, row),
        out_shape=(jax.ShapeDtypeStruct((S, D), F32), jax.ShapeDtypeStruct((S, D), h_dtype)),
        compiler_params=_params("parallel"),
        name="out_proj",
    )(x, y, w_out_bf, g.reshape(1, D))


def _encode(xb, norm_g, final_g, w_in, merge_b, rel_bias, hy_conv, hy_w1, hy_b1, hy_freq, hy_w2, hy_b2, hy_w3,
            hy_skip, diff_lam, diff_g, w_proj_bf, w_out_bf, tb):
    S = xb.shape[0]
    depth = w_in.shape[0]
    N1, N2 = tb["N1"], tb["N2"]
    h = _rmsnorm(xb, norm_g[0], BF16)
    for l in range(depth):
        qkv = [_in_proj(h, w_in, l, COL_A + g * A_GROUP_COLS, A_GROUP_COLS, dil=DIL_PAIRS[g][1]) for g in range(3)]
        p1 = _in_proj(h, w_in, l, COL_P1, P1_COLS)
        cqkv = _in_proj(h, w_in, l, COL_C, C_QKV_COLS, dil=1, scale0=C_Q_SCALE)
        p2 = _in_proj(h, w_in, l, COL_P2, P2_COLS)

        others = [_dilated_group(rel_bias, qkv[g], g, None, None) for g in (1, 2)]
        a_out = _dilated_group(rel_bias, qkv[0], 0, others, p1)

        u = _short_conv(p1, hy_conv, l, BRANCH_W)
        filt = _hyena_filters(S, hy_w1, hy_b1, hy_freq, hy_w2, hy_b2, hy_w3, hy_skip, l)
        fa_out = _outer_dft(tb["fa"], filt.reshape(HY_ORDER, N1, N2, BRANCH_W), None)
        hspec = _filter_spectrum(tb["gf"], fa_out.reshape(HY_ORDER, tb["NS"], 2 * N2, BRANCH_W))
        z = _long_conv(u, 0, u, 1, hspec, 0, tb)
        zb = _long_conv(z, 0, u, 2, hspec, 1, tb)

        c_out = _diff_attention(rel_bias, cqkv, diff_lam, diff_g, l, p2, 0)

        y = _merge(a_out, zb, p1, c_out, p2, w_proj_bf, merge_b, l)
        if l + 1 < depth:
            xb, h = _out_proj(xb, y, w_out_bf, l, norm_g[l + 1], BF16)
        else:
            _, h = _out_proj(xb, y, w_out_bf, l, final_g, F32)
    return h


def kernel(x, norm_g, final_g, w_in, merge_b, rel_bias, hy_conv, hy_w1, hy_b1, hy_freq, hy_w2, hy_b2, hy_w3,
           hy_skip, diff_lam, diff_g, w_proj, w_out):
    B, S, D = x.shape
    w_proj_bf = w_proj.astype(BF16)
    w_out_bf = w_out.astype(BF16)
    tb = _dft_tables(S)
    outs = [_encode(x[b], norm_g, final_g, w_in, merge_b, rel_bias, hy_conv, hy_w1, hy_b1, hy_freq, hy_w2, hy_b2,
                    hy_w3, hy_skip, diff_lam, diff_g, w_proj_bf, w_out_bf, tb) for b in range(B)]
    return outs[0].reshape(1, S, D) if B == 1 else jnp.stack(outs)
```

```python
import functools
import math

import jax
import jax.numpy as jnp
from jax import lax
from jax.experimental import pallas as pl
from jax.experimental.pallas import tpu as pltpu

F32 = jnp.float32
BF16 = jnp.bfloat16

LANES = 128
SUBLANES = 8
D_MODEL = 2048
BRANCH_W = 1024
N_BRANCH = 3
DIL_PAIRS = ((128, 1), (512, 4), (2048, 16))
A_HEADS = 8
A_HEAD_DIM = 128
REACH = 64
A_SUB = 128
A_KW = A_SUB + 2 * REACH
C_HEADS = 8
C_QK_DIM = 64
C_V_DIM = 128
HY_BANDS = 16
HY_FFN = 64
HY_ORDER = 2
HY_TARGET = 1e-2
HY_FAST_PCT = 0.3
HY_SLOW_PCT = 1.5
N_BUCKETS = 32
REL_MAX_DIST = 1024
NORM_EPS = 1e-6
NEG = -1e30

A_GROUP_COLS = 3 * A_HEADS * A_HEAD_DIM
A_QKV_COLS = 3 * A_GROUP_COLS
P1_COLS = BRANCH_W + 3 * BRANCH_W + BRANCH_W
C_QKV_COLS = 3 * C_HEADS * C_V_DIM
P2_COLS = BRANCH_W + N_BRANCH * D_MODEL
COL_A, COL_P1, COL_C, COL_P2 = 0, A_QKV_COLS, A_QKV_COLS + P1_COLS, A_QKV_COLS + P1_COLS + C_QKV_COLS

VMEM_LIMIT = 56 * 1024 * 1024
HIGHEST = lax.Precision.HIGHEST
LOG2E = math.log2(math.e)
C_Q_SCALE = LOG2E / math.sqrt(C_QK_DIM)


def _bucket_thresholds():
    nb = N_BUCKETS // 2
    max_exact = nb // 2
    thr = []
    for k in range(1, nb - max_exact):
        n = max_exact
        while int(math.log(n / max_exact) / math.log(REL_MAX_DIST / max_exact) * (nb - max_exact)) < k:
            n += 1
        thr.append(n)
    return tuple(thr)


BUCKET_THR = _bucket_thresholds()
FAR_DIST = BUCKET_THR[-1]


def _params(*sem):
    return pltpu.CompilerParams(dimension_semantics=sem, vmem_limit_bytes=VMEM_LIMIT)


def _rel_bias(rel, tab_ref, col):
    nb = N_BUCKETS // 2
    n = jnp.abs(rel)
    large = jnp.full(rel.shape, nb // 2, jnp.int32)
    for thr in BUCKET_THR:
        large = large + (n >= thr).astype(jnp.int32)
    bucket = jnp.where(n < nb // 2, n, large) + jnp.where(rel > 0, nb, 0)
    out = jnp.full(rel.shape, tab_ref[0, col], F32)
    for j in range(1, N_BUCKETS):
        out = jnp.where(bucket == j, tab_ref[j, col], out)
    return out


def _silu(x):
    return x * (1.0 / (1.0 + jnp.exp(-x)))


def _split_bf16(x):
    hi = x.astype(BF16)
    return hi, (x - hi.astype(F32)).astype(BF16)


def _const3(a):
    hi, lo = _split_bf16(a)
    return jnp.concatenate([hi, hi, lo], axis=-1)


def _dot3(a3, x):
    hi, lo = _split_bf16(x)
    return jnp.dot(a3, jnp.concatenate([hi, lo, hi], axis=0), preferred_element_type=F32)


def _dot3_both(a, x):
    return _dot3(_const3(a), x)


def _rmsnorm_kernel(x_ref, g_ref, o_ref):
    x = x_ref[...]
    ms = jnp.mean(x * x, axis=-1, keepdims=True)
    o_ref[...] = (x * lax.rsqrt(ms + NORM_EPS) * g_ref[...]).astype(o_ref.dtype)


def _rmsnorm(x, g, out_dtype, tm=512):
    S, D = x.shape
    return pl.pallas_call(
        _rmsnorm_kernel,
        grid=(S // tm,),
        in_specs=[pl.BlockSpec((tm, D), lambda i: (i, 0)), pl.BlockSpec((1, D), lambda i: (0, 0))],
        out_specs=pl.BlockSpec((tm, D), lambda i: (i, 0)),
        out_shape=jax.ShapeDtypeStruct((S, D), out_dtype),
        compiler_params=_params("parallel"),
        name="rmsnorm",
    )(x, g.reshape(1, D))


def _proj_kernel(h_ref, w_ref, o_ref, wb_ref, *acc_ref, dil, scale0):
    @pl.when(pl.program_id(1) == 0)
    def _():
        wb_ref[...] = w_ref[...].astype(BF16)

    acc = jnp.dot(h_ref[...], wb_ref[...], preferred_element_type=F32)
    if scale0 is not None:
        acc = acc * jnp.where(pl.program_id(0) == 0, scale0, 1.0)
    tm, tn = acc.shape
    if dil is None:
        o_ref[...] = acc
    elif dil == 1:
        for c in range(tn // LANES):
            o_ref[c, 0] = acc[:, c * LANES:(c + 1) * LANES].astype(o_ref.dtype)
    else:
        for c in range(tn // LANES):
            acc_ref[0][c] = acc[:, c * LANES:(c + 1) * LANES]
            for r in range(dil):
                o_ref[c, r] = acc_ref[0][c, pl.ds(r, tm // dil, stride=dil), :].astype(o_ref.dtype)


def _in_proj(h, w_in, layer, col0, ncols, dil=None, scale0=None, tm=1024, tn=1024):
    S, D = h.shape
    assert col0 % tn == 0 and ncols % tn == 0
    jb = col0 // tn
    scratch = [pltpu.VMEM((D, tn), BF16)]
    if dil is None:
        out_shape = jax.ShapeDtypeStruct((S, ncols), F32)
        out_spec = pl.BlockSpec((tm, tn), lambda j, i: (i, j))
    else:
        out_shape = jax.ShapeDtypeStruct((ncols // LANES, dil, S // dil, LANES), BF16)
        out_spec = pl.BlockSpec((tn // LANES, dil, tm // dil, LANES), lambda j, i: (j, 0, i, 0))
        if dil > 1:
            scratch.append(pltpu.VMEM((tn // LANES, tm, LANES), F32))
    return pl.pallas_call(
        functools.partial(_proj_kernel, dil=dil, scale0=scale0),
        grid=(ncols // tn, S // tm),
        in_specs=[pl.BlockSpec((tm, D), lambda j, i: (i, 0)),
                  pl.BlockSpec((None, D, tn), lambda j, i: (layer, 0, jb + j))],
        out_specs=out_spec,
        out_shape=out_shape,
        scratch_shapes=scratch,
        compiler_params=_params("parallel", "arbitrary"),
        name="in_proj",
    )(h, w_in)


def _dil_kernel(*refs, dil, col0, others, n, tq):
    it = iter(refs)
    tab_ref, q_ref, k_ref, v_ref = next(it), next(it), next(it), next(it)
    other_refs = [(next(it), next(it)) for _ in others]
    gate_ref = next(it) if others else None
    o_ref = next(it)
    lse_ref = None if others else next(it)
    bm_ref = next(it)
    pos_refs = [(next(it), next(it)) for _ in others]

    h, r, i = pl.program_id(0), pl.program_id(1), pl.program_id(2)
    deltas = (-REACH, 0, -2 * REACH)

    @pl.when((r == 0) & (i == 0))
    def _():
        t = lax.broadcasted_iota(jnp.int32, (A_SUB, A_KW), 0)
        u = lax.broadcasted_iota(jnp.int32, (A_SUB, A_KW), 1)
        for idx, delta in enumerate(deltas):
            rel = u + delta - t
            bias = _rel_bias(rel * dil, tab_ref, col0 + h)
            bm_ref[idx] = jnp.where(jnp.abs(rel) <= REACH, bias, NEG)

    for (src_o, src_l), (dst_o, dst_l), d in zip(other_refs, pos_refs, others):
        for rr in range(d):
            dst_o[pl.ds(rr, tq // d, stride=d), :] = src_o[rr]
            dst_l[pl.ds(rr, tq // d, stride=d), :] = src_l[rr]

    nsb = tq // A_SUB
    row_sl = [slice(sb * A_SUB, (sb + 1) * A_SUB) for sb in range(nsb)]
    m0s = [i * tq + sb * A_SUB for sb in range(nsb)]
    kss = [pl.multiple_of(jnp.clip(m0 - REACH, 0, n - A_KW), REACH) for m0 in m0s]
    sels = [jnp.where(m0 == 0, 1, jnp.where(m0 == n - A_SUB, 2, 0)) for m0 in m0s]
    ss = [lax.dot_general(q_ref[rows, :], k_ref[pl.ds(ks, A_KW), :], (((1,), (1,)), ((), ())),
                          preferred_element_type=F32) for rows, ks in zip(row_sl, kss)]
    ss = [s * (1.0 / math.sqrt(A_HEAD_DIM)) + bm_ref[sel] for s, sel in zip(ss, sels)]
    ms = [jnp.max(s, axis=-1, keepdims=True) for s in ss]
    ps = [jnp.exp(s - m) for s, m in zip(ss, ms)]
    ls = [jnp.sum(p, axis=-1, keepdims=True) for p in ps]
    pvs = [jnp.dot(p.astype(BF16), v_ref[pl.ds(ks, A_KW), :], preferred_element_type=F32) for p, ks in zip(ps, kss)]
    for rows, m, l, pv in zip(row_sl, ms, ls, pvs):
        o = pv / l
        lse = jnp.broadcast_to(m + jnp.log(l), o.shape)
        if others:
            lses = [lse] + [dst_l[rows, :] for _, dst_l in pos_refs]
            outs = [o] + [dst_o[rows, :] for dst_o, _ in pos_refs]
            mx = functools.reduce(jnp.maximum, lses)
            ws = [jnp.exp(x - mx) for x in lses]
            o = sum(w * x for w, x in zip(ws, outs)) / sum(ws)
            o_ref[rows, :] = (o * _silu(gate_ref[rows, :])).astype(o_ref.dtype)
        else:
            o_ref[rows, :] = o
            lse_ref[rows, :] = lse


def _dilated_group(tab, qkv, g, others, gate_src, tq=512):
    win, dil = DIL_PAIRS[g]
    assert win // (2 * dil) == REACH
    _, _, n, hd = qkv.shape
    tq = min(tq, n)
    assert qkv.shape[1] == dil and n % tq == 0 and n >= A_KW
    last = others is not None
    qblk = pl.BlockSpec((None, None, tq, hd), lambda h, r, i: (h, r, i, 0))
    in_specs = [pl.BlockSpec(memory_space=pltpu.SMEM), qblk,
                pl.BlockSpec((None, None, n, hd), lambda h, r, i: (A_HEADS + h, r, 0, 0)),
                pl.BlockSpec((None, None, n, hd), lambda h, r, i: (2 * A_HEADS + h, r, 0, 0))]
    args = [tab, qkv, qkv, qkv]
    scratch = [pltpu.VMEM((3, A_SUB, A_KW), F32)]
    other_dils = ()
    if last:
        assert dil == 1
        other_dils = tuple(o.shape[1] for o, _ in others)
        for (o, lse), d in zip(others, other_dils):
            spec = pl.BlockSpec((None, d, tq // d, hd), lambda h, r, i: (h, 0, i, 0))
            in_specs += [spec, spec]
            args += [o, lse]
            scratch += [pltpu.VMEM((tq, hd), F32), pltpu.VMEM((tq, hd), F32)]
        in_specs.append(pl.BlockSpec((tq, hd), lambda h, r, i: (i, h)))
        args.append(gate_src)
        out_shape = jax.ShapeDtypeStruct((n, A_HEADS * hd), BF16)
        out_specs = pl.BlockSpec((tq, hd), lambda h, r, i: (i, h))
    else:
        out_shape = (jax.ShapeDtypeStruct((A_HEADS, dil, n, hd), F32),) * 2
        out_specs = (qblk, qblk)
    return pl.pallas_call(
        functools.partial(_dil_kernel, dil=dil, col0=g * A_HEADS, others=other_dils, n=n, tq=tq),
        grid=(A_HEADS, dil, n // tq),
        in_specs=in_specs,
        out_specs=out_specs,
        out_shape=out_shape,
        scratch_shapes=scratch,
        compiler_params=_params("arbitrary", "arbitrary", "arbitrary"),
        name=f"dilated_attn_g{g}",
    )(*args)


def _diff_kernel(tab_ref, dl_ref, g_ref, q_ref, k_ref, v_ref, gate_ref, o_ref, strip_ref, v2_ref, *,
                 lam_init, S, tq, tk, nsub, dmin, dmax):
    h, i = pl.program_id(0), pl.program_id(1)
    width = strip_ref.shape[1]
    nk = S // tk
    tks = tk // nsub
    col = 3 * A_HEADS + h
    far_lo = tab_ref[N_BUCKETS // 2 - 1, col] * LOG2E
    far_hi = tab_ref[N_BUCKETS - 1, col] * LOG2E

    @pl.when(i == 0)
    def _():
        rows = 64
        for r0 in range(0, tq, rows):
            lo = max(0, (r0 - dmin - FAR_DIST) // LANES * LANES)
            hi = min(width, pl.cdiv(r0 + rows - dmin + FAR_DIST, LANES) * LANES)
            t = r0 + lax.broadcasted_iota(jnp.int32, (rows, hi - lo), 0)
            c = lo + lax.broadcasted_iota(jnp.int32, (rows, hi - lo), 1)
            strip_ref[r0:r0 + rows, lo:hi] = _rel_bias(c - t + dmin, tab_ref, col) * LOG2E
            if lo > 0:
                strip_ref[r0:r0 + rows, :lo] = jnp.full((rows, lo), far_lo, F32)
            if hi < width:
                strip_ref[r0:r0 + rows, hi:] = jnp.full((rows, width - hi), far_hi, F32)
        v2_ref[:, :C_V_DIM] = v_ref[...]
        v2_ref[:, C_V_DIM:] = jnp.ones((S, C_V_DIM), BF16)

    q = q_ref[...]
    lane = lax.broadcasted_iota(jnp.int32, q.shape, 1)
    zero = jnp.zeros_like(q)
    qs = (jnp.where(lane < C_QK_DIM, q, zero), jnp.where(lane >= C_QK_DIM, q, zero))

    def tile(j, carry, near):
        k0 = pl.multiple_of(j * tk, tk)
        if near:
            b0 = pl.multiple_of(j * tk - i * tq - dmin, LANES)
            shift = 0.0
        else:
            shift = jnp.where(j * tk < i * tq, far_lo, far_hi)
        ss = []
        for u in range(nsub):
            kt = k_ref[pl.ds(k0 + u * tks, tks), :]
            su = [lax.dot_general(qc, kt, (((1,), (1,)), ((), ())), preferred_element_type=F32) for qc in qs]
            if near:
                b = strip_ref[:, pl.ds(b0 + u * tks, tks)]
                su = [s + b for s in su]
            ss.append(su)
        for u in range(nsub):
            vt = v2_ref[pl.ds(k0 + u * tks, tks), :]
            ms = [jnp.maximum(m, jnp.max(s, axis=-1, keepdims=True) + shift) for s, (m, _) in zip(ss[u], carry)]
            ps = [jnp.exp2((s - (m_new - shift)).astype(BF16)) for s, m_new in zip(ss[u], ms)]
            pvs = [jnp.dot(p, vt, preferred_element_type=F32) for p in ps]
            carry = tuple((m_new, jnp.exp2(m - m_new) * acc + pv) for m_new, pv, (m, acc) in zip(ms, pvs, carry))
        return carry

    big = nk + 2
    j_lo = jnp.clip((dmin + i * tq + big * tk) // tk + 1 - big, 0, nk)
    j_hi = jnp.clip((dmax + i * tq + tk - 1) // tk, j_lo, nk)
    carry = ((jnp.full((tq, 1), NEG, F32), jnp.zeros((tq, 2 * C_V_DIM), F32)),) * 2
    carry = lax.fori_loop(0, j_lo, functools.partial(tile, near=False), carry)
    carry = lax.fori_loop(j_lo, j_hi, functools.partial(tile, near=True), carry)
    carry = lax.fori_loop(j_hi, nk, functools.partial(tile, near=False), carry)
    on = [acc[:, :C_V_DIM] / acc[:, C_V_DIM:] for _, acc in carry]
    dl = dl_ref[...]
    lam = (jnp.exp(jnp.sum(dl[0:1] * dl[1:2], axis=-1, keepdims=True))
           - jnp.exp(jnp.sum(dl[2:3] * dl[3:4], axis=-1, keepdims=True)) + lam_init)
    o = on[0] - lam * on[1]
    y = o * lax.rsqrt(jnp.mean(o * o, axis=-1, keepdims=True) + NORM_EPS) * g_ref[...]
    o_ref[...] = (y * (1.0 - lam_init) * _silu(gate_ref[...])).astype(o_ref.dtype)


def _diff_attention(tab, cqkv, diff_lam, diff_g, layer, gate_src, gate_col, tq=512, tk=2048, nsub=4):
    _, _, S, hd = cqkv.shape
    assert S % tk == 0 and S % tq == 0
    lam_init = 0.8 - 0.6 * math.exp(-0.3 * layer)
    dmax = pl.cdiv(FAR_DIST + tq - 1, LANES) * LANES
    dmin = -pl.cdiv(FAR_DIST + tk - 1, LANES) * LANES
    width = dmax - dmin + tk
    gcb = gate_col // hd
    return pl.pallas_call(
        functools.partial(_diff_kernel, lam_init=lam_init, S=S, tq=tq, tk=tk, nsub=nsub, dmin=dmin, dmax=dmax),
        grid=(C_HEADS, S // tq),
        in_specs=[pl.BlockSpec(memory_space=pltpu.SMEM),
                  pl.BlockSpec((None, 4, C_QK_DIM), lambda h, i: (layer, 0, 0)),
                  pl.BlockSpec((None, 1, C_V_DIM), lambda h, i: (layer, 0, 0)),
                  pl.BlockSpec((None, None, tq, hd), lambda h, i: (h, 0, i, 0)),
                  pl.BlockSpec((None, None, S, hd), lambda h, i: (C_HEADS + h, 0, 0, 0)),
                  pl.BlockSpec((None, None, S, hd), lambda h, i: (2 * C_HEADS + h, 0, 0, 0)),
                  pl.BlockSpec((tq, hd), lambda h, i: (i, gcb + h))],
        out_specs=pl.BlockSpec((tq, hd), lambda h, i: (i, h)),
        out_shape=jax.ShapeDtypeStruct((S, C_HEADS * hd), BF16),
        scratch_shapes=[pltpu.VMEM((tq, width), F32), pltpu.VMEM((S, 2 * C_V_DIM), BF16)],
        compiler_params=_params("arbitrary", "arbitrary"),
        name="diff_attn",
    )(tab, diff_lam, diff_g.reshape(diff_g.shape[0], 1, C_V_DIM), cqkv, cqkv, cqkv, gate_src)


def _short_conv_kernel(b_ref, w_ref, o_ref):
    b = b_ref[...]
    S = b.shape[0]
    row = lax.broadcasted_iota(jnp.int32, b.shape, 0)
    prev = jnp.where(row == 0, 0.0, pltpu.roll(b, 1, 0))
    nxt = jnp.where(row == S - 1, 0.0, pltpu.roll(b, S - 1, 0))
    w = w_ref[...]
    o_ref[...] = w[0:1] * prev + w[1:2] * b + w[2:3] * nxt


def _short_conv(p1, hy_conv, layer, col0):
    S = p1.shape[0]
    nct = BRANCH_W // LANES
    cb = col0 // LANES
    return pl.pallas_call(
        _short_conv_kernel,
        grid=(3 * nct,),
        in_specs=[pl.BlockSpec((S, LANES), lambda c: (0, cb + c)),
                  pl.BlockSpec((None, 3, LANES), lambda c: (layer, 0, c))],
        out_specs=pl.BlockSpec((None, S, LANES), lambda c: (c // nct, 0, c % nct)),
        out_shape=jax.ShapeDtypeStruct((3, S, BRANCH_W), F32),
        compiler_params=_params("parallel"),
        name="short_conv",
    )(p1, hy_conv)


def _filter_kernel(w1t_ref, w1c_ref, w1s_ref, b1_ref, fr_ref, w2_ref, b2_ref, w3a_ref, w3b_ref, w3a0_ref, w3b0_ref,
                   skip_ref, ad_ref, o_ref, *, L, tr):
    j = pl.program_id(0)
    N = 2 * L

    def hidden(pos):
        t = pos * (1.0 / (L - 1))
        w = pos * (2.0 * math.pi / L)
        fb = (1e-4 + lax.broadcasted_iota(jnp.int32, (HY_BANDS, 1), 0).astype(F32)
              * ((HY_BANDS - 1 - 1e-4) / (HY_BANDS - 1)))
        ang = fb * w
        pre = (w1t_ref[...] * t
               + jnp.dot(w1c_ref[...], jnp.cos(ang), precision=HIGHEST, preferred_element_type=F32)
               - jnp.dot(w1s_ref[...], jnp.sin(ang), precision=HIGHEST, preferred_element_type=F32)
               + b1_ref[...])
        fr = fr_ref[...]
        hid = jnp.sin(fr[:, 0:1] * pre)
        hid = jnp.sin(fr[:, 1:2] * (jnp.dot(w2_ref[...], hid, precision=HIGHEST, preferred_element_type=F32)
                                    + b2_ref[...]))
        return hid.T

    def position(shape, dim):
        row = j * tr + lax.broadcasted_iota(jnp.int32, shape, dim)
        return row, jnp.where(row < L, row, N - row).astype(F32)

    _, pos_lane = position((1, tr), 1)
    row, pos_sub = position((tr, 1), 0)
    hid = hidden(pos_lane)
    decay = jnp.exp(-(pos_sub * (1.0 / (L - 1))) * ad_ref[...])
    for o, w3_ref in enumerate((w3a_ref, w3b_ref)):
        g = _dot3_both(hid, w3_ref[...]) * decay
        o_ref[o] = jnp.where(row == L, 0.0, g)

    @pl.when(j == 0)
    def _():
        hid0 = hidden(jnp.zeros((1, LANES), F32))[0:SUBLANES]
        r8 = lax.broadcasted_iota(jnp.int32, (SUBLANES, 1), 0)
        for o, w3_ref in enumerate((w3a0_ref, w3b0_ref)):
            hb0 = _dot3_both(hid0, w3_ref[...])
            o_ref[o, 0:SUBLANES, :] = o_ref[o, 0:SUBLANES, :] + jnp.where(r8 == 0, hb0 + skip_ref[o], 0.0)


def _hyena_filters(L, hy_w1, hy_b1, hy_freq, hy_w2, hy_b2, hy_w3, hy_skip, layer, tr=512):
    assert HY_ORDER == 2
    N = 2 * L
    W = BRANCH_W
    nb = HY_BANDS
    w1 = hy_w1[layer].T
    max_decay = math.log(HY_TARGET) / HY_FAST_PCT
    min_decay = math.log(HY_TARGET) / HY_SLOW_PCT
    absdelta = jnp.abs(jnp.linspace(min_decay, max_decay, W, dtype=F32)).reshape(1, W)
    full = lambda shape: pl.BlockSpec(shape, lambda j: (0,) * len(shape))
    nfwd = L // tr
    w3_spec = lambda o, bwd: pl.BlockSpec(
        (None, HY_FFN, W), lambda j: (layer, 0, 2 * o + ((j >= nfwd).astype(jnp.int32) if bwd is None else bwd)))
    return pl.pallas_call(
        functools.partial(_filter_kernel, L=L, tr=tr),
        grid=(N // tr,),
        in_specs=[full((HY_FFN, 1)), full((HY_FFN, nb)), full((HY_FFN, nb)), full((HY_FFN, 1)), full((HY_FFN, 2)),
                  full((HY_FFN, HY_FFN)), full((HY_FFN, 1)),
                  w3_spec(0, None), w3_spec(1, None), w3_spec(0, 1), w3_spec(1, 1),
                  pl.BlockSpec((None, HY_ORDER, 1, W), lambda j: (layer, 0, 0, 0)),
                  full((1, W))],
        out_specs=pl.BlockSpec((HY_ORDER, tr, W), lambda j: (0, j, 0)),
        out_shape=jax.ShapeDtypeStruct((HY_ORDER, N, W), F32),
        compiler_params=_params("arbitrary"),
        name="hyena_filter",
    )(w1[:, 0:1], w1[:, 1:1 + nb], w1[:, 1 + nb:], hy_b1[layer].reshape(HY_FFN, 1), hy_freq[layer].T,
      hy_w2[layer].T, hy_b2[layer].reshape(HY_FFN, 1), hy_w3, hy_w3, hy_w3, hy_w3,
      hy_skip.reshape(hy_skip.shape[0], HY_ORDER, 1, W), absdelta)


def _outer_dft_kernel(f_ref, x_ref, *rest, gated):
    gate_ref = rest[0] if gated else None
    o_ref, xs_ref, ys_ref = rest[-3:]
    K, sub, tw = x_ref.shape
    M = o_ref.shape[0]
    nch = tw // LANES
    for c in range(nch):
        xs_ref[c] = x_ref[:, :, c * LANES:(c + 1) * LANES].reshape(K * sub, LANES)
    f3 = f_ref[...]
    for s in range(sub):
        x = jnp.concatenate([xs_ref[c, pl.ds(s, K, stride=sub), :] for c in range(nch)], axis=1)
        y = _dot3(f3, x)
        for c in range(nch):
            ys_ref[c, pl.ds(s, M, stride=sub), :] = y[:, c * LANES:(c + 1) * LANES]
    for c in range(nch):
        y = ys_ref[c].reshape(M, sub, LANES)
        if gated:
            y = y * gate_ref[:, :, c * LANES:(c + 1) * LANES]
        o_ref[:, :, c * LANES:(c + 1) * LANES] = y


def _outer_dft(f3, x, xb, gate=None, gb=0, tw=BRANCH_W):
    _, K, N2, W = x.shape
    M = f3.shape[0]
    B = x.shape[0] if xb is None else 1
    x_idx = (lambda b: b) if xb is None else (lambda b: xb)
    blk = lambda rows, idx: pl.BlockSpec((None, rows, SUBLANES, tw), lambda b, t, c: (idx(b), 0, t, c))
    in_specs = [pl.BlockSpec((M, 3 * K), lambda b, t, c: (0, 0)), blk(K, x_idx)]
    args = [f3, x]
    if gate is not None:
        in_specs.append(blk(M, lambda b: gb))
        args.append(gate)
    return pl.pallas_call(
        functools.partial(_outer_dft_kernel, gated=gate is not None),
        grid=(B, N2 // SUBLANES, W // tw),
        in_specs=in_specs,
        out_specs=blk(M, lambda b: b),
        out_shape=jax.ShapeDtypeStruct((B, M, N2, W), F32),
        scratch_shapes=[pltpu.VMEM((tw // LANES, K * SUBLANES, LANES), F32),
                        pltpu.VMEM((tw // LANES, M * SUBLANES, LANES), F32)],
        compiler_params=_params("parallel", "parallel", "parallel"),
        name="dft_outer",
    )(*args)


def _spectrum_kernel(gf_ref, a_ref, o_ref):
    o_ref[...] = _dot3(gf_ref[...], a_ref[...])


def _filter_spectrum(gf3, a):
    O, N1, R, W = a.shape
    return pl.pallas_call(
        _spectrum_kernel,
        grid=(O, N1),
        in_specs=[pl.BlockSpec((None, R, 3 * R), lambda o, k: (k, 0, 0)),
                  pl.BlockSpec((None, None, R, W), lambda o, k: (o, k, 0, 0))],
        out_specs=pl.BlockSpec((None, None, R, W), lambda o, k: (o, k, 0, 0)),
        out_shape=jax.ShapeDtypeStruct(a.shape, F32),
        compiler_params=_params("parallel", "parallel"),
        name="filter_spectrum",
    )(gf3, a)


def _conv_mid_kernel(gf_ref, gi_ref, a_ref, h_ref, o_ref):
    half = a_ref.shape[0] // 2
    x = _dot3(gf_ref[...], a_ref[...])
    xr, xi = x[:half], x[half:]
    hr, hi = h_ref[:half, :], h_ref[half:, :]
    y = jnp.concatenate([xr * hr - xi * hi, xr * hi + xi * hr], axis=0)
    o_ref[...] = _dot3(gi_ref[...], y)


def _conv_mid(gf3, gi3, a, hspec, order):
    N1, R, W = a.shape
    return pl.pallas_call(
        _conv_mid_kernel,
        grid=(N1,),
        in_specs=[pl.BlockSpec((None, R, 3 * R), lambda k: (k, 0, 0)),
                  pl.BlockSpec((None, R, 3 * R), lambda k: (k, 0, 0)),
                  pl.BlockSpec((None, R, W), lambda k: (k, 0, 0)),
                  pl.BlockSpec((None, None, R, W), lambda k: (order, k, 0, 0))],
        out_specs=pl.BlockSpec((None, R, W), lambda k: (k, 0, 0)),
        out_shape=jax.ShapeDtypeStruct(a.shape, F32),
        compiler_params=_params("parallel"),
        name="conv_mid",
    )(gf3, gi3, a, hspec)


def _dft_tables(L):
    N = 2 * L
    N2 = LANES
    N1 = N // N2
    NS = pl.cdiv(N1 // 2 + 1, SUBLANES) * SUBLANES
    two_pi = 2.0 * math.pi
    ks = jnp.arange(NS, dtype=jnp.int32)
    t1 = jnp.arange(N1, dtype=jnp.int32)
    live = (ks <= N1 // 2).astype(F32)
    ang = ((ks[:, None] * t1[None, :]) % N1).astype(F32) * (two_pi / N1)
    cs = jnp.stack([jnp.cos(ang), -jnp.sin(ang)], axis=1) * live[:, None, None]
    fa = cs.reshape(2 * NS, N1)
    weight = jnp.where((ks == 0) | (ks == N1 // 2), 1.0, 2.0) / N
    fi = jnp.transpose(cs * weight[:, None, None], (2, 0, 1)).reshape(N1, 2 * NS)[:N1 // 2]
    t2 = jnp.arange(N2, dtype=jnp.int32)
    k = ks[:, None, None] + N1 * t2[None, :, None]
    ph = ((k * t2[None, None, :]) % N).astype(F32) * (two_pi / N)
    gr, gim = jnp.cos(ph), -jnp.sin(ph)
    gf = jnp.concatenate([jnp.concatenate([gr, -gim], axis=2), jnp.concatenate([gim, gr], axis=2)], axis=1)
    gi = jnp.swapaxes(gf, 1, 2)
    return {"fa": _const3(fa), "fh": _const3(fa[:, :N1 // 2]), "fi": _const3(fi), "gf": _const3(gf),
            "gi": _const3(gi), "N1": N1, "N2": N2, "NS": NS}


def _long_conv(z, zi, gate, gi_, hspec, order, tb):
    _, S, W = z.shape
    N1, N2, NS = tb["N1"], tb["N2"], tb["NS"]
    split = lambda t: t.reshape(t.shape[0], N1 // 2, N2, W)
    a = _outer_dft(tb["fh"], split(z), zi)
    b = _conv_mid(tb["gf"], tb["gi"], a.reshape(NS, 2 * N2, W), hspec, order)
    y = _outer_dft(tb["fi"], b.reshape(1, 2 * NS, N2, W), 0, split(gate), gi_)
    return y.reshape(1, S, W)


def _merge_kernel(a_ref, zb_ref, bg_ref, c_ref, w_ref, ma_ref, mb_ref, mc_ref, bias_ref, o_ref):
    b = (zb_ref[...] * _silu(bg_ref[...])).astype(BF16)
    bias = bias_ref[...]
    y = None
    for n, (br, m_ref) in enumerate(((a_ref[...], ma_ref), (b, mb_ref), (c_ref[...], mc_ref))):
        pb = jnp.dot(br, w_ref[n], preferred_element_type=F32)
        gate = 1.0 / (1.0 + jnp.exp(-(m_ref[...] + bias[n:n + 1])))
        y = gate * pb if y is None else y + gate * pb
    o_ref[...] = y.astype(o_ref.dtype)


def _merge(a_out, zb, p1, c_out, p2, w_proj_bf, merge_b, layer, tm=512, tn=1024):
    S = a_out.shape[0]
    W, D = BRANCH_W, D_MODEL
    bgb = (P1_COLS - W) // W
    mcb = W // tn
    branch = pl.BlockSpec((tm, W), lambda i, j: (i, 0))
    mspec = lambda n: pl.BlockSpec((tm, tn), lambda i, j: (i, mcb + n * (D // tn) + j))
    return pl.pallas_call(
        _merge_kernel,
        grid=(S // tm, D // tn),
        in_specs=[branch, pl.BlockSpec((None, tm, W), lambda i, j: (0, i, 0)),
                  pl.BlockSpec((tm, W), lambda i, j: (i, bgb)), branch,
                  pl.BlockSpec((None, N_BRANCH, W, tn), lambda i, j: (layer, 0, 0, j)),
                  mspec(0), mspec(1), mspec(2),
                  pl.BlockSpec((None, N_BRANCH, tn), lambda i, j: (layer, 0, j))],
        out_specs=pl.BlockSpec((tm, tn), lambda i, j: (i, j)),
        out_shape=jax.ShapeDtypeStruct((S, D), BF16),
        compiler_params=_params("parallel", "arbitrary"),
        name="merge",
    )(a_out, zb, p1, c_out, w_proj_bf, p2, p2, p2, merge_b)


def _out_kernel(x_ref, y_ref, w_ref, g_ref, xo_ref, ho_ref):
    x = x_ref[...] + jnp.dot(y_ref[...], w_ref[...], preferred_element_type=F32)
    xo_ref[...] = x
    ms = jnp.mean(x * x, axis=-1, keepdims=True)
    ho_ref[...] = (x * lax.rsqrt(ms + NORM_EPS) * g_ref[...]).astype(ho_ref.dtype)


def _out_proj(x, y, w_out_bf, layer, g, h_dtype, tm=512):
    S, D = x.shape
    row = pl.BlockSpec((tm, D), lambda i: (i, 0))
    return pl.pallas_call(
        _out_kernel,
        grid=(S // tm,),
        in_specs=[row, row, pl.BlockSpec((None, D, D), lambda i: (layer, 0, 0)), pl.BlockSpec((1, D), lambda i: (0, 0))],
        out_specs=(row, row),
        out_shape=(jax.ShapeDtypeStruct((S, D), F32), jax.ShapeDtypeStruct((S, D), h_dtype)),
        compiler_params=_params("parallel"),
        name="out_proj",
    )(x, y, w_out_bf, g.reshape(1, D))


def _encode(xb, norm_g, final_g, w_in, merge_b, rel_bias, hy_conv, hy_w1, hy_b1, hy_freq, hy_w2, hy_b2, hy_w3,
            hy_skip, diff_lam, diff_g, w_proj_bf, w_out_bf, tb):
    S = xb.shape[0]
    depth = w_in.shape[0]
    N1, N2 = tb["N1"], tb["N2"]
    h = _rmsnorm(xb, norm_g[0], BF16)
    for l in range(depth):
        qkv = [_in_proj(h, w_in, l, COL_A + g * A_GROUP_COLS, A_GROUP_COLS, dil=DIL_PAIRS[g][1]) for g in range(3)]
        p1 = _in_proj(h, w_in, l, COL_P1, P1_COLS)
        cqkv = _in_proj(h, w_in, l, COL_C, C_QKV_COLS, dil=1, scale0=C_Q_SCALE)
        p2 = _in_proj(h, w_in, l, COL_P2, P2_COLS)

        others = [_dilated_group(rel_bias, qkv[g], g, None, None) for g in (1, 2)]
        a_out = _dilated_group(rel_bias, qkv[0], 0, others, p1)

        u = _short_conv(p1, hy_conv, l, BRANCH_W)
        filt = _hyena_filters(S, hy_w1, hy_b1, hy_freq, hy_w2, hy_b2, hy_w3, hy_skip, l)
        fa_out = _outer_dft(tb["fa"], filt.reshape(HY_ORDER, N1, N2, BRANCH_W), None)
        hspec = _filter_spectrum(tb["gf"], fa_out.reshape(HY_ORDER, tb["NS"], 2 * N2, BRANCH_W))
        z = _long_conv(u, 0, u, 1, hspec, 0, tb)
        zb = _long_conv(z, 0, u, 2, hspec, 1, tb)

        c_out = _diff_attention(rel_bias, cqkv, diff_lam, diff_g, l, p2, 0)

        y = _merge(a_out, zb, p1, c_out, p2, w_proj_bf, merge_b, l)
        if l + 1 < depth:
            xb, h = _out_proj(xb, y, w_out_bf, l, norm_g[l + 1], BF16)
        else:
            _, h = _out_proj(xb, y, w_out_bf, l, final_g, F32)
    return h


def kernel(x, norm_g, final_g, w_in, merge_b, rel_bias, hy_conv, hy_w1, hy_b1, hy_freq, hy_w2, hy_b2, hy_w3,
           hy_skip, diff_lam, diff_g, w_proj, w_out):
    B, S, D = x.shape
    w_proj_bf = w_proj.astype(BF16)
    w_out_bf = w_out.astype(BF16)
    tb = _dft_tables(S)
    outs = [_encode(x[b], norm_g, final_g, w_in, merge_b, rel_bias, hy_conv, hy_w1, hy_b1, hy_freq, hy_w2, hy_b2,
                    hy_w3, hy_skip, diff_lam, diff_g, w_proj_bf, w_out_bf, tb) for b in range(B)]
    return outs[0].reshape(1, S, D) if B == 1 else jnp.stack(outs)
```

```python
import functools
import math

import jax
import jax.numpy as jnp
from jax import lax
from jax.experimental import pallas as pl
from jax.experimental.pallas import tpu as pltpu

F32 = jnp.float32
BF16 = jnp.bfloat16

LANES = 128
SUBLANES = 8
D_MODEL = 2048
BRANCH_W = 1024
N_BRANCH = 3
DIL_PAIRS = ((128, 1), (512, 4), (2048, 16))
A_HEADS = 8
A_HEAD_DIM = 128
REACH = 64
A_SUB = 128
A_KW = A_SUB + 2 * REACH
A_CHAINS = 8
C_HEADS = 8
C_QK_DIM = 64
C_V_DIM = 128
HY_BANDS = 16
HY_FFN = 64
HY_ORDER = 2
HY_TARGET = 1e-2
HY_FAST_PCT = 0.3
HY_SLOW_PCT = 1.5
N_BUCKETS = 32
REL_MAX_DIST = 1024
NORM_EPS = 1e-6
NEG = -1e30

A_GROUP_COLS = 3 * A_HEADS * A_HEAD_DIM
A_QKV_COLS = 3 * A_GROUP_COLS
P1_COLS = BRANCH_W + 3 * BRANCH_W + BRANCH_W
C_QKV_COLS = 3 * C_HEADS * C_V_DIM
P2_COLS = BRANCH_W + N_BRANCH * D_MODEL
COL_A, COL_P1, COL_C, COL_P2 = 0, A_QKV_COLS, A_QKV_COLS + P1_COLS, A_QKV_COLS + P1_COLS + C_QKV_COLS

VMEM_LIMIT = 56 * 1024 * 1024
HIGHEST = lax.Precision.HIGHEST
LOG2E = math.log2(math.e)
C_Q_SCALE = LOG2E / math.sqrt(C_QK_DIM)

def _bucket_thresholds():
    nb = N_BUCKETS // 2
    max_exact = nb // 2
    thr = []
    for k in range(1, nb - max_exact):
        n = max_exact
        while int(math.log(n / max_exact) / math.log(REL_MAX_DIST / max_exact) * (nb - max_exact)) < k:
            n += 1
        thr.append(n)
    return tuple(thr)


BUCKET_THR = _bucket_thresholds()
FAR_DIST = BUCKET_THR[-1]


def _params(*sem):
    return pltpu.CompilerParams(dimension_semantics=sem, vmem_limit_bytes=VMEM_LIMIT)


def _rel_bias(rel, tab_ref, col):
    nb = N_BUCKETS // 2
    n = jnp.abs(rel)
    large = jnp.full(rel.shape, nb // 2, jnp.int32)
    for thr in BUCKET_THR:
        large = large + (n >= thr).astype(jnp.int32)
    bucket = jnp.where(n < nb // 2, n, large) + jnp.where(rel > 0, nb, 0)
    out = jnp.full(rel.shape, tab_ref[0, col], F32)
    for j in range(1, N_BUCKETS):
        out = jnp.where(bucket == j, tab_ref[j, col], out)
    return out


def _sigmoid(x):
    return 0.5 * jnp.tanh(0.5 * x) + 0.5


def _silu(x):
    return x * _sigmoid(x)


def _split_bf16(x):
    hi = x.astype(BF16)
    return hi, (x - hi.astype(F32)).astype(BF16)


def _const3(a):
    hi, lo = _split_bf16(a)
    return jnp.concatenate([hi, hi, lo], axis=-1)


def _dot3(a3, x):
    hi, lo = _split_bf16(x)
    return jnp.dot(a3, jnp.concatenate([hi, lo, hi], axis=0), preferred_element_type=F32)


def _dot3_both(a, x):
    return _dot3(_const3(a), x)


def _rmsnorm_kernel(x_ref, g_ref, o_ref):
    x = x_ref[...]
    ms = jnp.mean(x * x, axis=-1, keepdims=True)
    o_ref[...] = (x * lax.rsqrt(ms + NORM_EPS) * g_ref[...]).astype(o_ref.dtype)


def _rmsnorm(x, g, out_dtype, tm=512):
    S, D = x.shape
    return pl.pallas_call(
        _rmsnorm_kernel,
        grid=(S // tm,),
        in_specs=[pl.BlockSpec((tm, D), lambda i: (i, 0)), pl.BlockSpec((1, D), lambda i: (0, 0))],
        out_specs=pl.BlockSpec((tm, D), lambda i: (i, 0)),
        out_shape=jax.ShapeDtypeStruct((S, D), out_dtype),
        compiler_params=_params("parallel"),
        name="rmsnorm",
    )(x, g.reshape(1, D))


def _proj_kernel(h_ref, w_ref, o_ref, wb_ref, *acc_ref, dil, scale0):
    @pl.when(pl.program_id(1) == 0)
    def _():
        wb_ref[...] = w_ref[...].astype(BF16)

    acc = jnp.dot(h_ref[...], wb_ref[...], preferred_element_type=F32)
    if scale0 is not None:
        acc = acc * jnp.where(pl.program_id(0) == 0, scale0, 1.0)
    tm, tn = acc.shape
    if dil is None:
        o_ref[...] = acc
    elif dil == 1:
        for c in range(tn // LANES):
            o_ref[c, 0] = acc[:, c * LANES:(c + 1) * LANES].astype(o_ref.dtype)
    else:
        for c in range(tn // LANES):
            acc_ref[0][c] = acc[:, c * LANES:(c + 1) * LANES]
            for r in range(dil):
                o_ref[c, r] = acc_ref[0][c, pl.ds(r, tm // dil, stride=dil), :].astype(o_ref.dtype)


def _in_proj(h, w_in, layer, col0, ncols, dil=None, scale0=None, tm=1024, tn=1024):
    S, D = h.shape
    assert col0 % tn == 0 and ncols % tn == 0
    jb = col0 // tn
    scratch = [pltpu.VMEM((D, tn), BF16)]
    if dil is None:
        out_shape = jax.ShapeDtypeStruct((S, ncols), F32)
        out_spec = pl.BlockSpec((tm, tn), lambda j, i: (i, j))
    else:
        out_shape = jax.ShapeDtypeStruct((ncols // LANES, dil, S // dil, LANES), BF16)
        out_spec = pl.BlockSpec((tn // LANES, dil, tm // dil, LANES), lambda j, i: (j, 0, i, 0))
        if dil > 1:
            scratch.append(pltpu.VMEM((tn // LANES, tm, LANES), F32))
    return pl.pallas_call(
        functools.partial(_proj_kernel, dil=dil, scale0=scale0),
        grid=(ncols // tn, S // tm),
        in_specs=[pl.BlockSpec((tm, D), lambda j, i: (i, 0)),
                  pl.BlockSpec((None, D, tn), lambda j, i: (layer, 0, jb + j))],
        out_specs=out_spec,
        out_shape=out_shape,
        scratch_shapes=scratch,
        compiler_params=_params("parallel", "arbitrary"),
        name="in_proj",
    )(h, w_in)


def _dil_kernel(*refs, dil, col0, others, n, tq):
    it = iter(refs)
    tab_ref, q_ref, k_ref, v_ref = next(it), next(it), next(it), next(it)
    other_refs = [(next(it), next(it)) for _ in others]
    gate_ref = next(it) if others else None
    o_ref = next(it)
    lse_ref = None if others else next(it)
    bm_ref = next(it)
    pos_refs = [(next(it), next(it)) for _ in others]

    h, r, i = pl.program_id(0), pl.program_id(1), pl.program_id(2)
    deltas = (-REACH, 0, -2 * REACH)

    @pl.when((r == 0) & (i == 0))
    def _():
        t = lax.broadcasted_iota(jnp.int32, (A_SUB, A_KW), 0)
        u = lax.broadcasted_iota(jnp.int32, (A_SUB, A_KW), 1)
        for idx, delta in enumerate(deltas):
            rel = u + delta - t
            bias = _rel_bias(rel * dil, tab_ref, col0 + h)
            bm_ref[idx] = jnp.where(jnp.abs(rel) <= REACH, bias, NEG)

    for (src_o, src_l), (dst_o, dst_l), d in zip(other_refs, pos_refs, others):
        for rr in range(d):
            dst_o[pl.ds(rr, tq // d, stride=d), :] = src_o[rr]
            dst_l[pl.ds(rr, tq // d, stride=d), :] = src_l[rr]

    rb = q_ref.shape[0]
    subs = [(rr, sb) for rr in range(rb) for sb in range(tq // A_SUB)]
    row_sl = [slice(sb * A_SUB, (sb + 1) * A_SUB) for _, sb in subs]
    m0s = [i * tq + sb * A_SUB for _, sb in subs]
    kss = [pl.multiple_of(jnp.clip(m0 - REACH, 0, n - A_KW), REACH) for m0 in m0s]
    sels = [jnp.where(m0 == 0, 1, jnp.where(m0 == n - A_SUB, 2, 0)) for m0 in m0s]
    ss = [lax.dot_general(q_ref[rr, rows, :], k_ref[rr, pl.ds(ks, A_KW), :], (((1,), (1,)), ((), ())),
                          preferred_element_type=F32) for (rr, _), rows, ks in zip(subs, row_sl, kss)]
    ss = [s * (1.0 / math.sqrt(A_HEAD_DIM)) + bm_ref[sel] for s, sel in zip(ss, sels)]
    ms = [jnp.max(s, axis=-1, keepdims=True) for s in ss]
    ps = [jnp.exp(s - m) for s, m in zip(ss, ms)]
    ls = [jnp.sum(p, axis=-1, keepdims=True) for p in ps]
    pvs = [jnp.dot(p.astype(BF16), v_ref[rr, pl.ds(ks, A_KW), :], preferred_element_type=F32)
           for (rr, _), p, ks in zip(subs, ps, kss)]
    for (rr, _), rows, m, l, pv in zip(subs, row_sl, ms, ls, pvs):
        o = pv / l
        lse = jnp.broadcast_to(m + jnp.log(l), o.shape)
        if others:
            lses = [lse] + [dst_l[rows, :] for _, dst_l in pos_refs]
            outs = [o] + [dst_o[rows, :] for dst_o, _ in pos_refs]
            mx = functools.reduce(jnp.maximum, lses)
            ws = [jnp.exp(x - mx) for x in lses]
            o = sum(w * x for w, x in zip(ws, outs)) / sum(ws)
            o_ref[rows, :] = (o * _silu(gate_ref[rows, :])).astype(o_ref.dtype)
        else:
            o_ref[rr, rows, :] = o
            lse_ref[rr, rows, :] = lse


def _dilated_group(tab, qkv, g, others, gate_src, tq=512):
    win, dil = DIL_PAIRS[g]
    assert win // (2 * dil) == REACH
    _, _, n, hd = qkv.shape
    tq = min(tq, n)
    rb = min(dil, A_CHAINS * A_SUB // tq)
    assert qkv.shape[1] == dil and n % tq == 0 and n >= A_KW and dil % rb == 0
    last = others is not None
    qblk = pl.BlockSpec((None, rb, tq, hd), lambda h, r, i: (h, r, i, 0))
    in_specs = [pl.BlockSpec(memory_space=pltpu.SMEM), qblk,
                pl.BlockSpec((None, rb, n, hd), lambda h, r, i: (A_HEADS + h, r, 0, 0)),
                pl.BlockSpec((None, rb, n, hd), lambda h, r, i: (2 * A_HEADS + h, r, 0, 0))]
    args = [tab, qkv, qkv, qkv]
    scratch = [pltpu.VMEM((3, A_SUB, A_KW), F32)]
    other_dils = ()
    if last:
        assert dil == 1
        other_dils = tuple(o.shape[1] for o, _ in others)
        for (o, lse), d in zip(others, other_dils):
            spec = pl.BlockSpec((None, d, tq // d, hd), lambda h, r, i: (h, 0, i, 0))
            in_specs += [spec, spec]
            args += [o, lse]
            scratch += [pltpu.VMEM((tq, hd), F32), pltpu.VMEM((tq, hd), F32)]
        in_specs.append(pl.BlockSpec((tq, hd), lambda h, r, i: (i, h)))
        args.append(gate_src)
        out_shape = jax.ShapeDtypeStruct((n, A_HEADS * hd), BF16)
        out_specs = pl.BlockSpec((tq, hd), lambda h, r, i: (i, h))
    else:
        out_shape = (jax.ShapeDtypeStruct((A_HEADS, dil, n, hd), F32),) * 2
        out_specs = (qblk, qblk)
    return pl.pallas_call(
        functools.partial(_dil_kernel, dil=dil, col0=g * A_HEADS, others=other_dils, n=n, tq=tq),
        grid=(A_HEADS, dil // rb, n // tq),
        in_specs=in_specs,
        out_specs=out_specs,
        out_shape=out_shape,
        scratch_shapes=scratch,
        compiler_params=_params("arbitrary", "arbitrary", "arbitrary"),
        name=f"dilated_attn_g{g}",
    )(*args)


def _diff_kernel(tab_ref, dl_ref, g_ref, q_ref, k_ref, v_ref, gate_ref, o_ref, strip_ref, v2_ref, *,
                 lam_init, S, tq, tk, nsub, dmin, dmax):
    h, i = pl.program_id(0), pl.program_id(1)
    width = strip_ref.shape[1]
    nk = S // tk
    tks = tk // nsub
    col = 3 * A_HEADS + h
    far_lo = tab_ref[N_BUCKETS // 2 - 1, col] * LOG2E
    far_hi = tab_ref[N_BUCKETS - 1, col] * LOG2E

    @pl.when(i == 0)
    def _():
        rows = 64
        for r0 in range(0, tq, rows):
            lo = max(0, (r0 - dmin - FAR_DIST) // LANES * LANES)
            hi = min(width, pl.cdiv(r0 + rows - dmin + FAR_DIST, LANES) * LANES)
            t = r0 + lax.broadcasted_iota(jnp.int32, (rows, hi - lo), 0)
            c = lo + lax.broadcasted_iota(jnp.int32, (rows, hi - lo), 1)
            strip_ref[r0:r0 + rows, lo:hi] = _rel_bias(c - t + dmin, tab_ref, col) * LOG2E
            if lo > 0:
                strip_ref[r0:r0 + rows, :lo] = jnp.full((rows, lo), far_lo, F32)
            if hi < width:
                strip_ref[r0:r0 + rows, hi:] = jnp.full((rows, width - hi), far_hi, F32)
        v2_ref[:, :C_V_DIM] = v_ref[...]
        v2_ref[:, C_V_DIM:] = jnp.ones((S, C_V_DIM), BF16)

    q = q_ref[...]
    lane = lax.broadcasted_iota(jnp.int32, q.shape, 1)
    zero = jnp.zeros_like(q)
    qs = (jnp.where(lane < C_QK_DIM, q, zero), jnp.where(lane >= C_QK_DIM, q, zero))

    def logits(j, u):
        k0 = pl.multiple_of(j * tk, tk)
        b0 = pl.multiple_of(jnp.clip(j * tk - i * tq, dmin, dmax) - dmin, LANES)
        kt = k_ref[pl.ds(k0 + u * tks, tks), :]
        b = strip_ref[:, pl.ds(b0 + u * tks, tks)]
        return [lax.dot_general(qc, kt, (((1,), (1,)), ((), ())), preferred_element_type=F32) + b for qc in qs]

    def values(j, u):
        return v2_ref[pl.ds(pl.multiple_of(j * tk, tk) + u * tks, tks), :]

    def exact_tile(j, carry):
        look = 1
        queue = [logits(j, u) for u in range(min(look, nsub))]
        for u in range(nsub):
            if u + look < nsub:
                queue.append(logits(j, u + look))
            ss = queue.pop(0)
            ms = [jnp.maximum(m, jnp.max(s, axis=-1, keepdims=True)) for s, (m, _) in zip(ss, carry)]
            ps = [jnp.exp2((s - m_new).astype(BF16)) for s, m_new in zip(ss, ms)]
            pvs = [jnp.dot(p, values(j, u), preferred_element_type=F32) for p in ps]
            carry = tuple((m_new, jnp.exp2(m - m_new) * acc + pv) for m_new, pv, (m, acc) in zip(ms, pvs, carry))
        return carry

    start = (jnp.full((tq, 1), NEG, F32), jnp.zeros((tq, 2 * C_V_DIM), F32))
    carry = lax.fori_loop(0, nk, exact_tile, (start, start))
    on = [acc[:, :C_V_DIM] / acc[:, C_V_DIM:] for _, acc in carry]
    dl = dl_ref[...]
    lam = (jnp.exp(jnp.sum(dl[0:1] * dl[1:2], axis=-1, keepdims=True))
           - jnp.exp(jnp.sum(dl[2:3] * dl[3:4], axis=-1, keepdims=True)) + lam_init)
    o = on[0] - lam * on[1]
    y = o * lax.rsqrt(jnp.mean(o * o, axis=-1, keepdims=True) + NORM_EPS) * g_ref[...]
    o_ref[...] = (y * (1.0 - lam_init) * _silu(gate_ref[...])).astype(o_ref.dtype)


def _diff_attention(tab, cqkv, diff_lam, diff_g, layer, gate_src, gate_col, tq=512, tk=2048, nsub=4):
    _, _, S, hd = cqkv.shape
    assert S % tk == 0 and S % tq == 0
    lam_init = 0.8 - 0.6 * math.exp(-0.3 * layer)
    dmax = pl.cdiv(FAR_DIST + tq - 1, LANES) * LANES
    dmin = -pl.cdiv(FAR_DIST + tk - 1, LANES) * LANES
    width = dmax - dmin + tk
    gcb = gate_col // hd
    return pl.pallas_call(
        functools.partial(_diff_kernel, lam_init=lam_init, S=S, tq=tq, tk=tk, nsub=nsub, dmin=dmin, dmax=dmax),
        grid=(C_HEADS, S // tq),
        in_specs=[pl.BlockSpec(memory_space=pltpu.SMEM),
                  pl.BlockSpec((None, 4, C_QK_DIM), lambda h, i: (layer, 0, 0)),
                  pl.BlockSpec((None, 1, C_V_DIM), lambda h, i: (layer, 0, 0)),
                  pl.BlockSpec((None, None, tq, hd), lambda h, i: (h, 0, i, 0)),
                  pl.BlockSpec((None, None, S, hd), lambda h, i: (C_HEADS + h, 0, 0, 0)),
                  pl.BlockSpec((None, None, S, hd), lambda h, i: (2 * C_HEADS + h, 0, 0, 0)),
                  pl.BlockSpec((tq, hd), lambda h, i: (i, gcb + h))],
        out_specs=pl.BlockSpec((tq, hd), lambda h, i: (i, h)),
        out_shape=jax.ShapeDtypeStruct((S, C_HEADS * hd), BF16),
        scratch_shapes=[pltpu.VMEM((tq, width), F32), pltpu.VMEM((S, 2 * C_V_DIM), BF16)],
        compiler_params=_params("arbitrary", "arbitrary"),
        name="diff_attn",
    )(tab, diff_lam, diff_g.reshape(diff_g.shape[0], 1, C_V_DIM), cqkv, cqkv, cqkv, gate_src)


def _short_conv_kernel(b_ref, w_ref, o_ref):
    b = b_ref[...]
    S = b.shape[0]
    row = lax.broadcasted_iota(jnp.int32, b.shape, 0)
    prev = jnp.where(row == 0, 0.0, pltpu.roll(b, 1, 0))
    nxt = jnp.where(row == S - 1, 0.0, pltpu.roll(b, S - 1, 0))
    w = w_ref[...]
    o_ref[...] = w[0:1] * prev + w[1:2] * b + w[2:3] * nxt


def _short_conv(p1, hy_conv, layer, col0):
    S = p1.shape[0]
    nct = BRANCH_W // LANES
    cb = col0 // LANES
    return pl.pallas_call(
        _short_conv_kernel,
        grid=(3 * nct,),
        in_specs=[pl.BlockSpec((S, LANES), lambda c: (0, cb + c)),
                  pl.BlockSpec((None, 3, LANES), lambda c: (layer, 0, c))],
        out_specs=pl.BlockSpec((None, S, LANES), lambda c: (c // nct, 0, c % nct)),
        out_shape=jax.ShapeDtypeStruct((3, S, BRANCH_W), F32),
        compiler_params=_params("parallel"),
        name="short_conv",
    )(p1, hy_conv)


def _filter_kernel(w1t_ref, w1c_ref, w1s_ref, b1_ref, fr_ref, w2_ref, b2_ref, w3a_ref, w3b_ref, w3a0_ref, w3b0_ref,
                   skip_ref, ad_ref, o_ref, *, L, tr):
    j = pl.program_id(0)
    N = 2 * L

    def hidden(pos):
        t = pos * (1.0 / (L - 1))
        w = pos * (2.0 * math.pi / L)
        fb = (1e-4 + lax.broadcasted_iota(jnp.int32, (HY_BANDS, 1), 0).astype(F32)
              * ((HY_BANDS - 1 - 1e-4) / (HY_BANDS - 1)))
        ang = fb * w
        pre = (w1t_ref[...] * t
               + jnp.dot(w1c_ref[...], jnp.cos(ang), precision=HIGHEST, preferred_element_type=F32)
               - jnp.dot(w1s_ref[...], jnp.sin(ang), precision=HIGHEST, preferred_element_type=F32)
               + b1_ref[...])
        fr = fr_ref[...]
        hid = jnp.sin(fr[:, 0:1] * pre)
        hid = jnp.sin(fr[:, 1:2] * (jnp.dot(w2_ref[...], hid, precision=HIGHEST, preferred_element_type=F32)
                                    + b2_ref[...]))
        return hid.T

    def position(shape, dim):
        row = j * tr + lax.broadcasted_iota(jnp.int32, shape, dim)
        return row, jnp.where(row < L, row, N - row).astype(F32)

    _, pos_lane = position((1, tr), 1)
    row, pos_sub = position((tr, 1), 0)
    hid = hidden(pos_lane)
    decay = jnp.exp(-(pos_sub * (1.0 / (L - 1))) * ad_ref[...])
    for o, w3_ref in enumerate((w3a_ref, w3b_ref)):
        g = _dot3_both(hid, w3_ref[...]) * decay
        o_ref[o] = jnp.where(row == L, 0.0, g)

    @pl.when(j == 0)
    def _():
        hid0 = hidden(jnp.zeros((1, LANES), F32))[0:SUBLANES]
        r8 = lax.broadcasted_iota(jnp.int32, (SUBLANES, 1), 0)
        for o, w3_ref in enumerate((w3a0_ref, w3b0_ref)):
            hb0 = _dot3_both(hid0, w3_ref[...])
            o_ref[o, 0:SUBLANES, :] = o_ref[o, 0:SUBLANES, :] + jnp.where(r8 == 0, hb0 + skip_ref[o], 0.0)


def _hyena_filters(L, hy_w1, hy_b1, hy_freq, hy_w2, hy_b2, hy_w3, hy_skip, layer, tr=512):
    assert HY_ORDER == 2
    N = 2 * L
    W = BRANCH_W
    nb = HY_BANDS
    w1 = hy_w1[layer].T
    max_decay = math.log(HY_TARGET) / HY_FAST_PCT
    min_decay = math.log(HY_TARGET) / HY_SLOW_PCT
    absdelta = jnp.abs(jnp.linspace(min_decay, max_decay, W, dtype=F32)).reshape(1, W)
    full = lambda shape: pl.BlockSpec(shape, lambda j: (0,) * len(shape))
    nfwd = L // tr
    w3_spec = lambda o, bwd: pl.BlockSpec(
        (None, HY_FFN, W), lambda j: (layer, 0, 2 * o + ((j >= nfwd).astype(jnp.int32) if bwd is None else bwd)))
    return pl.pallas_call(
        functools.partial(_filter_kernel, L=L, tr=tr),
        grid=(N // tr,),
        in_specs=[full((HY_FFN, 1)), full((HY_FFN, nb)), full((HY_FFN, nb)), full((HY_FFN, 1)), full((HY_FFN, 2)),
                  full((HY_FFN, HY_FFN)), full((HY_FFN, 1)),
                  w3_spec(0, None), w3_spec(1, None), w3_spec(0, 1), w3_spec(1, 1),
                  pl.BlockSpec((None, HY_ORDER, 1, W), lambda j: (layer, 0, 0, 0)),
                  full((1, W))],
        out_specs=pl.BlockSpec((HY_ORDER, tr, W), lambda j: (0, j, 0)),
        out_shape=jax.ShapeDtypeStruct((HY_ORDER, N, W), F32),
        compiler_params=_params("arbitrary"),
        name="hyena_filter",
    )(w1[:, 0:1], w1[:, 1:1 + nb], w1[:, 1 + nb:], hy_b1[layer].reshape(HY_FFN, 1), hy_freq[layer].T,
      hy_w2[layer].T, hy_b2[layer].reshape(HY_FFN, 1), hy_w3, hy_w3, hy_w3, hy_w3,
      hy_skip.reshape(hy_skip.shape[0], HY_ORDER, 1, W), absdelta)


def _outer_dft_kernel(f_ref, x_ref, *rest, gated):
    gate_ref = rest[0] if gated else None
    o_ref, xs_ref, ys_ref = rest[-3:]
    K, sub, tw = x_ref.shape
    M = o_ref.shape[0]
    nch = tw // LANES
    for c in range(nch):
        xs_ref[c] = x_ref[:, :, c * LANES:(c + 1) * LANES].reshape(K * sub, LANES)
    f3 = f_ref[...]
    for s in range(sub):
        x = jnp.concatenate([xs_ref[c, pl.ds(s, K, stride=sub), :] for c in range(nch)], axis=1)
        y = _dot3(f3, x)
        for c in range(nch):
            ys_ref[c, pl.ds(s, M, stride=sub), :] = y[:, c * LANES:(c + 1) * LANES]
    for c in range(nch):
        y = ys_ref[c].reshape(M, sub, LANES)
        if gated:
            y = y * gate_ref[:, :, c * LANES:(c + 1) * LANES]
        o_ref[:, :, c * LANES:(c + 1) * LANES] = y


def _outer_dft(f3, x, xb, gate=None, gb=0, tw=BRANCH_W):
    _, K, N2, W = x.shape
    M = f3.shape[0]
    B = x.shape[0] if xb is None else 1
    x_idx = (lambda b: b) if xb is None else (lambda b: xb)
    blk = lambda rows, idx: pl.BlockSpec((None, rows, SUBLANES, tw), lambda b, t, c: (idx(b), 0, t, c))
    in_specs = [pl.BlockSpec((M, 3 * K), lambda b, t, c: (0, 0)), blk(K, x_idx)]
    args = [f3, x]
    if gate is not None:
        in_specs.append(blk(M, lambda b: gb))
        args.append(gate)
    return pl.pallas_call(
        functools.partial(_outer_dft_kernel, gated=gate is not None),
        grid=(B, N2 // SUBLANES, W // tw),
        in_specs=in_specs,
        out_specs=blk(M, lambda b: b),
        out_shape=jax.ShapeDtypeStruct((B, M, N2, W), F32),
        scratch_shapes=[pltpu.VMEM((tw // LANES, K * SUBLANES, LANES), F32),
                        pltpu.VMEM((tw // LANES, M * SUBLANES, LANES), F32)],
        compiler_params=_params("parallel", "parallel", "parallel"),
        name="dft_outer",
    )(*args)


def _conv_mid_kernel(gf_ref, gi_ref, a_ref, f_ref, o_ref):
    half = a_ref.shape[0] // 2
    gf3 = gf_ref[...]
    x = _dot3(gf3, a_ref[...])
    h = _dot3(gf3, f_ref[...])
    xr, xi = x[:half], x[half:]
    hr, hi = h[:half], h[half:]
    y = jnp.concatenate([xr * hr - xi * hi, xr * hi + xi * hr], axis=0)
    o_ref[...] = _dot3(gi_ref[...], y)


def _conv_mid(gf3, gi3, a, hspec, order):
    N1, R, W = a.shape
    return pl.pallas_call(
        _conv_mid_kernel,
        grid=(N1,),
        in_specs=[pl.BlockSpec((None, R, 3 * R), lambda k: (k, 0, 0)),
                  pl.BlockSpec((None, R, 3 * R), lambda k: (k, 0, 0)),
                  pl.BlockSpec((None, R, W), lambda k: (k, 0, 0)),
                  pl.BlockSpec((None, None, R, W), lambda k: (order, k, 0, 0))],
        out_specs=pl.BlockSpec((None, R, W), lambda k: (k, 0, 0)),
        out_shape=jax.ShapeDtypeStruct(a.shape, F32),
        compiler_params=_params("parallel"),
        name="conv_mid",
    )(gf3, gi3, a, hspec)


def _dft_tables(L):
    N = 2 * L
    N2 = LANES
    N1 = N // N2
    NS = pl.cdiv(N1 // 2 + 1, SUBLANES) * SUBLANES
    two_pi = 2.0 * math.pi
    ks = jnp.arange(NS, dtype=jnp.int32)
    t1 = jnp.arange(N1, dtype=jnp.int32)
    live = (ks <= N1 // 2).astype(F32)
    ang = ((ks[:, None] * t1[None, :]) % N1).astype(F32) * (two_pi / N1)
    cs = jnp.stack([jnp.cos(ang), -jnp.sin(ang)], axis=1) * live[:, None, None]
    fa = cs.reshape(2 * NS, N1)
    weight = jnp.where((ks == 0) | (ks == N1 // 2), 1.0, 2.0) / N
    fi = jnp.transpose(cs * weight[:, None, None], (2, 0, 1)).reshape(N1, 2 * NS)[:N1 // 2]
    t2 = jnp.arange(N2, dtype=jnp.int32)
    k = ks[:, None, None] + N1 * t2[None, :, None]
    ph = ((k * t2[None, None, :]) % N).astype(F32) * (two_pi / N)
    gr, gim = jnp.cos(ph), -jnp.sin(ph)
    gf = jnp.concatenate([jnp.concatenate([gr, -gim], axis=2), jnp.concatenate([gim, gr], axis=2)], axis=1)
    gi = jnp.swapaxes(gf, 1, 2)
    return {"fa": _const3(fa), "fh": _const3(fa[:, :N1 // 2]), "fi": _const3(fi), "gf": _const3(gf),
            "gi": _const3(gi), "N1": N1, "N2": N2, "NS": NS}


def _long_conv(z, zi, gate, gi_, hspec, order, tb):
    _, S, W = z.shape
    N1, N2, NS = tb["N1"], tb["N2"], tb["NS"]
    split = lambda t: t.reshape(t.shape[0], N1 // 2, N2, W)
    a = _outer_dft(tb["fh"], split(z), zi)
    b = _conv_mid(tb["gf"], tb["gi"], a.reshape(NS, 2 * N2, W), hspec, order)
    y = _outer_dft(tb["fi"], b.reshape(1, 2 * NS, N2, W), 0, split(gate), gi_)
    return y.reshape(1, S, W)


def _merge_kernel(a_ref, zb_ref, bg_ref, c_ref, w_ref, ma_ref, mb_ref, mc_ref, bias_ref, o_ref):
    b = (zb_ref[...] * _silu(bg_ref[...])).astype(BF16)
    bias = bias_ref[...]
    y = None
    for n, (br, m_ref) in enumerate(((a_ref[...], ma_ref), (b, mb_ref), (c_ref[...], mc_ref))):
        pb = jnp.dot(br, w_ref[n], preferred_element_type=F32)
        gate = _sigmoid(m_ref[...] + bias[n:n + 1])
        y = gate * pb if y is None else y + gate * pb
    o_ref[...] = y.astype(o_ref.dtype)


def _merge(a_out, zb, p1, c_out, p2, w_proj_bf, merge_b, layer, tm=512, tn=1024):
    S = a_out.shape[0]
    W, D = BRANCH_W, D_MODEL
    bgb = (P1_COLS - W) // W
    mcb = W // tn
    branch = pl.BlockSpec((tm, W), lambda i, j: (i, 0))
    mspec = lambda n: pl.BlockSpec((tm, tn), lambda i, j: (i, mcb + n * (D // tn) + j))
    return pl.pallas_call(
        _merge_kernel,
        grid=(S // tm, D // tn),
        in_specs=[branch, pl.BlockSpec((None, tm, W), lambda i, j: (0, i, 0)),
                  pl.BlockSpec((tm, W), lambda i, j: (i, bgb)), branch,
                  pl.BlockSpec((None, N_BRANCH, W, tn), lambda i, j: (layer, 0, 0, j)),
                  mspec(0), mspec(1), mspec(2),
                  pl.BlockSpec((None, N_BRANCH, tn), lambda i, j: (layer, 0, j))],
        out_specs=pl.BlockSpec((tm, tn), lambda i, j: (i, j)),
        out_shape=jax.ShapeDtypeStruct((S, D), BF16),
        compiler_params=_params("parallel", "arbitrary"),
        name="merge",
    )(a_out, zb, p1, c_out, w_proj_bf, p2, p2, p2, merge_b)


def _out_kernel(x_ref, y_ref, w_ref, g_ref, xo_ref, ho_ref):
    x = x_ref[...] + jnp.dot(y_ref[...], w_ref[...], preferred_element_type=F32)
    xo_ref[...] = x
    ms = jnp.mean(x * x, axis=-1, keepdims=True)
    ho_ref[...] = (x * lax.rsqrt(ms + NORM_EPS) * g_ref[...]).astype(ho_ref.dtype)


def _out_proj(x, y, w_out_bf, layer, g, h_dtype, tm=512):
    S, D = x.shape
    row = pl.BlockSpec((tm, D), lambda i: (i, 0))
    return pl.pallas_call(
        _out_kernel,
        grid=(S // tm,),
        in_specs=[row, row, pl.BlockSpec((None, D, D), lambda i: (layer, 0, 0)), pl.BlockSpec((1, D), lambda i: (0, 0))],
        out_specs=(row, row),
        out_shape=(jax.ShapeDtypeStruct((S, D), F32), jax.ShapeDtypeStruct((S, D), h_dtype)),
        compiler_params=_params("parallel"),
        name="out_proj",
    )(x, y, w_out_bf, g.reshape(1, D))


def _encode(xb, norm_g, final_g, w_in, merge_b, rel_bias, hy_conv, hy_w1, hy_b1, hy_freq, hy_w2, hy_b2, hy_w3,
            hy_skip, diff_lam, diff_g, w_proj_bf, w_out_bf, tb):
    S = xb.shape[0]
    depth = w_in.shape[0]
    N1, N2 = tb["N1"], tb["N2"]
    h = _rmsnorm(xb, norm_g[0], BF16)
    for l in range(depth):
        qkv = [_in_proj(h, w_in, l, COL_A + g * A_GROUP_COLS, A_GROUP_COLS, dil=DIL_PAIRS[g][1]) for g in range(3)]
        p1 = _in_proj(h, w_in, l, COL_P1, P1_COLS)
        cqkv = _in_proj(h, w_in, l, COL_C, C_QKV_COLS, dil=1, scale0=C_Q_SCALE)
        p2 = _in_proj(h, w_in, l, COL_P2, P2_COLS)

        others = [_dilated_group(rel_bias, qkv[g], g, None, None) for g in (1, 2)]
        a_out = _dilated_group(rel_bias, qkv[0], 0, others, p1, tq=A_CHAINS * A_SUB)

        u = _short_conv(p1, hy_conv, l, BRANCH_W)
        filt = _hyena_filters(S, hy_w1, hy_b1, hy_freq, hy_w2, hy_b2, hy_w3, hy_skip, l)
        fa_out = _outer_dft(tb["fa"], filt.reshape(HY_ORDER, N1, N2, BRANCH_W), None)
        hspec = fa_out.reshape(HY_ORDER, tb["NS"], 2 * N2, BRANCH_W)
        z = _long_conv(u, 0, u, 1, hspec, 0, tb)
        zb = _long_conv(z, 0, u, 2, hspec, 1, tb)

        c_out = _diff_attention(rel_bias, cqkv, diff_lam, diff_g, l, p2, 0)

        y = _merge(a_out, zb, p1, c_out, p2, w_proj_bf, merge_b, l)
        if l + 1 < depth:
            xb, h = _out_proj(xb, y, w_out_bf, l, norm_g[l + 1], BF16)
        else:
            _, h = _out_proj(xb, y, w_out_bf, l, final_g, F32)
    return h


def kernel(x, norm_g, final_g, w_in, merge_b, rel_bias, hy_conv, hy_w1, hy_b1, hy_freq, hy_w2, hy_b2, hy_w3,
           hy_skip, diff_lam, diff_g, w_proj, w_out):
    B, S, D = x.shape
    w_proj_bf = w_proj.astype(BF16)
    w_out_bf = w_out.astype(BF16)
    tb = _dft_tables(S)
    outs = [_encode(x[b], norm_g, final_g, w_in, merge_b, rel_bias, hy_conv, hy_w1, hy_b1, hy_freq, hy_w2, hy_b2,
                    hy_w3, hy_skip, diff_lam, diff_g, w_proj_bf, w_out_bf, tb) for b in range(B)]
    return outs[0].reshape(1, S, D) if B == 1 else jnp.stack(outs)
```

```python
import functools
import math

import jax
import jax.numpy as jnp
from jax import lax
from jax.experimental import pallas as pl
from jax.experimental.pallas import tpu as pltpu

F32 = jnp.float32
BF16 = jnp.bfloat16

LANES = 128
SUBLANES = 8
D_MODEL = 2048
BRANCH_W = 1024
N_BRANCH = 3
DIL_PAIRS = ((128, 1), (512, 4), (2048, 16))
A_HEADS = 8
A_HEAD_DIM = 128
REACH = 64
A_SUB = 128
A_KW = A_SUB + 2 * REACH
A_CHAINS = 8
C_HEADS = 8
C_QK_DIM = 64
C_V_DIM = 128
HY_BANDS = 16
HY_FFN = 64
HY_ORDER = 2
HY_TARGET = 1e-2
HY_FAST_PCT = 0.3
HY_SLOW_PCT = 1.5
N_BUCKETS = 32
REL_MAX_DIST = 1024
NORM_EPS = 1e-6
NEG = -1e30

A_GROUP_COLS = 3 * A_HEADS * A_HEAD_DIM
A_QKV_COLS = 3 * A_GROUP_COLS
P1_COLS = BRANCH_W + 3 * BRANCH_W + BRANCH_W
C_QKV_COLS = 3 * C_HEADS * C_V_DIM
P2_COLS = BRANCH_W + N_BRANCH * D_MODEL
COL_A, COL_P1, COL_C, COL_P2 = 0, A_QKV_COLS, A_QKV_COLS + P1_COLS, A_QKV_COLS + P1_COLS + C_QKV_COLS

VMEM_LIMIT = 56 * 1024 * 1024
HIGHEST = lax.Precision.HIGHEST
LOG2E = math.log2(math.e)
C_Q_SCALE = LOG2E / math.sqrt(C_QK_DIM)

def _bucket_thresholds():
    nb = N_BUCKETS // 2
    max_exact = nb // 2
    thr = []
    for k in range(1, nb - max_exact):
        n = max_exact
        while int(math.log(n / max_exact) / math.log(REL_MAX_DIST / max_exact) * (nb - max_exact)) < k:
            n += 1
        thr.append(n)
    return tuple(thr)


BUCKET_THR = _bucket_thresholds()
FAR_DIST = BUCKET_THR[-1]


def _params(*sem):
    return pltpu.CompilerParams(dimension_semantics=sem, vmem_limit_bytes=VMEM_LIMIT)


def _rel_bias(rel, tab_ref, col):
    nb = N_BUCKETS // 2
    n = jnp.abs(rel)
    large = jnp.full(rel.shape, nb // 2, jnp.int32)
    for thr in BUCKET_THR:
        large = large + (n >= thr).astype(jnp.int32)
    bucket = jnp.where(n < nb // 2, n, large) + jnp.where(rel > 0, nb, 0)
    out = jnp.full(rel.shape, tab_ref[0, col], F32)
    for j in range(1, N_BUCKETS):
        out = jnp.where(bucket == j, tab_ref[j, col], out)
    return out


def _sigmoid(x):
    return 0.5 * jnp.tanh(0.5 * x) + 0.5


def _silu(x):
    return x * _sigmoid(x)


def _split_bf16(x):
    hi = x.astype(BF16)
    return hi, (x - hi.astype(F32)).astype(BF16)


def _const3(a):
    hi, lo = _split_bf16(a)
    return jnp.concatenate([hi, hi, lo], axis=-1)


def _dot3(a3, x):
    hi, lo = _split_bf16(x)
    return jnp.dot(a3, jnp.concatenate([hi, lo, hi], axis=0), preferred_element_type=F32)


def _dot3_both(a, x):
    return _dot3(_const3(a), x)


def _rmsnorm_kernel(x_ref, g_ref, o_ref):
    x = x_ref[...]
    ms = jnp.mean(x * x, axis=-1, keepdims=True)
    o_ref[...] = (x * lax.rsqrt(ms + NORM_EPS) * g_ref[...]).astype(o_ref.dtype)


def _rmsnorm(x, g, out_dtype, tm=512):
    S, D = x.shape
    return pl.pallas_call(
        _rmsnorm_kernel,
        grid=(S // tm,),
        in_specs=[pl.BlockSpec((tm, D), lambda i: (i, 0)), pl.BlockSpec((1, D), lambda i: (0, 0))],
        out_specs=pl.BlockSpec((tm, D), lambda i: (i, 0)),
        out_shape=jax.ShapeDtypeStruct((S, D), out_dtype),
        compiler_params=_params("parallel"),
        name="rmsnorm",
    )(x, g.reshape(1, D))


def _proj_kernel(h_ref, w_ref, o_ref, wb_ref, *acc_ref, dil, scale0):
    @pl.when(pl.program_id(1) == 0)
    def _():
        wb_ref[...] = w_ref[...].astype(BF16)

    acc = jnp.dot(h_ref[...], wb_ref[...], preferred_element_type=F32)
    if scale0 is not None:
        acc = acc * jnp.where(pl.program_id(0) == 0, scale0, 1.0)
    tm, tn = acc.shape
    if dil is None:
        o_ref[...] = acc
    elif dil == 1:
        for c in range(tn // LANES):
            o_ref[c, 0] = acc[:, c * LANES:(c + 1) * LANES].astype(o_ref.dtype)
    else:
        for c in range(tn // LANES):
            acc_ref[0][c] = acc[:, c * LANES:(c + 1) * LANES]
            for r in range(dil):
                o_ref[c, r] = acc_ref[0][c, pl.ds(r, tm // dil, stride=dil), :].astype(o_ref.dtype)


def _in_proj(h, w_in, layer, col0, ncols, dil=None, scale0=None, tm=1024, tn=1024):
    S, D = h.shape
    assert col0 % tn == 0 and ncols % tn == 0
    jb = col0 // tn
    scratch = [pltpu.VMEM((D, tn), BF16)]
    if dil is None:
        out_shape = jax.ShapeDtypeStruct((S, ncols), F32)
        out_spec = pl.BlockSpec((tm, tn), lambda j, i: (i, j))
    else:
        out_shape = jax.ShapeDtypeStruct((ncols // LANES, dil, S // dil, LANES), BF16)
        out_spec = pl.BlockSpec((tn // LANES, dil, tm // dil, LANES), lambda j, i: (j, 0, i, 0))
        if dil > 1:
            scratch.append(pltpu.VMEM((tn // LANES, tm, LANES), F32))
    return pl.pallas_call(
        functools.partial(_proj_kernel, dil=dil, scale0=scale0),
        grid=(ncols // tn, S // tm),
        in_specs=[pl.BlockSpec((tm, D), lambda j, i: (i, 0)),
                  pl.BlockSpec((None, D, tn), lambda j, i: (layer, 0, jb + j))],
        out_specs=out_spec,
        out_shape=out_shape,
        scratch_shapes=scratch,
        compiler_params=_params("parallel", "arbitrary"),
        name="in_proj",
    )(h, w_in)


def _dil_kernel(*refs, dil, col0, others, n, tq):
    it = iter(refs)
    tab_ref, q_ref, k_ref, v_ref = next(it), next(it), next(it), next(it)
    other_refs = [(next(it), next(it)) for _ in others]
    gate_ref = next(it) if others else None
    o_ref = next(it)
    lse_ref = None if others else next(it)
    bm_ref = next(it)
    pos_refs = [(next(it), next(it)) for _ in others]

    h, r, i = pl.program_id(0), pl.program_id(1), pl.program_id(2)
    deltas = (-REACH, 0, -2 * REACH)

    @pl.when((r == 0) & (i == 0))
    def _():
        t = lax.broadcasted_iota(jnp.int32, (A_SUB, A_KW), 0)
        u = lax.broadcasted_iota(jnp.int32, (A_SUB, A_KW), 1)
        for idx, delta in enumerate(deltas):
            rel = u + delta - t
            bias = _rel_bias(rel * dil, tab_ref, col0 + h)
            bm_ref[idx] = jnp.where(jnp.abs(rel) <= REACH, bias, NEG)

    for (src_o, src_l), (dst_o, dst_l), d in zip(other_refs, pos_refs, others):
        for rr in range(d):
            dst_o[pl.ds(rr, tq // d, stride=d), :] = src_o[rr]
            dst_l[pl.ds(rr, tq // d, stride=d), :] = src_l[rr]

    rb = q_ref.shape[0]
    subs = [(rr, sb) for rr in range(rb) for sb in range(tq // A_SUB)]
    row_sl = [slice(sb * A_SUB, (sb + 1) * A_SUB) for _, sb in subs]
    m0s = [i * tq + sb * A_SUB for _, sb in subs]
    kss = [pl.multiple_of(jnp.clip(m0 - REACH, 0, n - A_KW), REACH) for m0 in m0s]
    sels = [jnp.where(m0 == 0, 1, jnp.where(m0 == n - A_SUB, 2, 0)) for m0 in m0s]
    ss = [lax.dot_general(q_ref[rr, rows, :], k_ref[rr, pl.ds(ks, A_KW), :], (((1,), (1,)), ((), ())),
                          preferred_element_type=F32) for (rr, _), rows, ks in zip(subs, row_sl, kss)]
    ss = [s * (1.0 / math.sqrt(A_HEAD_DIM)) + bm_ref[sel] for s, sel in zip(ss, sels)]
    ms = [jnp.max(s, axis=-1, keepdims=True) for s in ss]
    ps = [jnp.exp(s - m) for s, m in zip(ss, ms)]
    ls = [jnp.sum(p, axis=-1, keepdims=True) for p in ps]
    pvs = [jnp.dot(p.astype(BF16), v_ref[rr, pl.ds(ks, A_KW), :], preferred_element_type=F32)
           for (rr, _), p, ks in zip(subs, ps, kss)]
    for (rr, _), rows, m, l, pv in zip(subs, row_sl, ms, ls, pvs):
        o = pv / l
        lse = jnp.broadcast_to(m + jnp.log(l), o.shape)
        if others:
            lses = [lse] + [dst_l[rows, :] for _, dst_l in pos_refs]
            outs = [o] + [dst_o[rows, :] for dst_o, _ in pos_refs]
            mx = functools.reduce(jnp.maximum, lses)
            ws = [jnp.exp(x - mx) for x in lses]
            o = sum(w * x for w, x in zip(ws, outs)) / sum(ws)
            o_ref[rows, :] = (o * _silu(gate_ref[rows, :])).astype(o_ref.dtype)
        else:
            o_ref[rr, rows, :] = o
            lse_ref[rr, rows, :] = lse


def _dilated_group(tab, qkv, g, others, gate_src, tq=512):
    win, dil = DIL_PAIRS[g]
    assert win // (2 * dil) == REACH
    _, _, n, hd = qkv.shape
    tq = min(tq, n)
    rb = min(dil, A_CHAINS * A_SUB // tq)
    assert qkv.shape[1] == dil and n % tq == 0 and n >= A_KW and dil % rb == 0
    last = others is not None
    qblk = pl.BlockSpec((None, rb, tq, hd), lambda h, r, i: (h, r, i, 0))
    in_specs = [pl.BlockSpec(memory_space=pltpu.SMEM), qblk,
                pl.BlockSpec((None, rb, n, hd), lambda h, r, i: (A_HEADS + h, r, 0, 0)),
                pl.BlockSpec((None, rb, n, hd), lambda h, r, i: (2 * A_HEADS + h, r, 0, 0))]
    args = [tab, qkv, qkv, qkv]
    scratch = [pltpu.VMEM((3, A_SUB, A_KW), F32)]
    other_dils = ()
    if last:
        assert dil == 1
        other_dils = tuple(o.shape[1] for o, _ in others)
        for (o, lse), d in zip(others, other_dils):
            spec = pl.BlockSpec((None, d, tq // d, hd), lambda h, r, i: (h, 0, i, 0))
            in_specs += [spec, spec]
            args += [o, lse]
            scratch += [pltpu.VMEM((tq, hd), F32), pltpu.VMEM((tq, hd), F32)]
        in_specs.append(pl.BlockSpec((tq, hd), lambda h, r, i: (i, h)))
        args.append(gate_src)
        out_shape = jax.ShapeDtypeStruct((n, A_HEADS * hd), BF16)
        out_specs = pl.BlockSpec((tq, hd), lambda h, r, i: (i, h))
    else:
        out_shape = (jax.ShapeDtypeStruct((A_HEADS, dil, n, hd), F32),) * 2
        out_specs = (qblk, qblk)
    return pl.pallas_call(
        functools.partial(_dil_kernel, dil=dil, col0=g * A_HEADS, others=other_dils, n=n, tq=tq),
        grid=(A_HEADS, dil // rb, n // tq),
        in_specs=in_specs,
        out_specs=out_specs,
        out_shape=out_shape,
        scratch_shapes=scratch,
        compiler_params=_params("arbitrary", "arbitrary", "arbitrary"),
        name=f"dilated_attn_g{g}",
    )(*args)


def _strip_kernel(tab_ref, o_ref, *, dmin, rows):
    h, r = pl.program_id(0), pl.program_id(1)
    width = o_ref.shape[1]
    col = 3 * A_HEADS + h
    far_lo = tab_ref[N_BUCKETS // 2 - 1, col] * LOG2E
    far_hi = tab_ref[N_BUCKETS - 1, col] * LOG2E
    r0 = r * rows
    band = pl.cdiv(rows + 2 * FAR_DIST, LANES) * LANES + LANES
    lo = pl.multiple_of(jnp.clip((r0 - dmin - FAR_DIST) // LANES * LANES, 0, width - band), LANES)
    c_all = lax.broadcasted_iota(jnp.int32, (rows, width), 1)
    t_all = r0 + lax.broadcasted_iota(jnp.int32, (rows, width), 0)
    o_ref[...] = jnp.where(c_all - t_all + dmin < 0, far_lo, far_hi)
    t = r0 + lax.broadcasted_iota(jnp.int32, (rows, band), 0)
    c = lo + lax.broadcasted_iota(jnp.int32, (rows, band), 1)
    o_ref[:, pl.ds(lo, band)] = _rel_bias(c - t + dmin, tab_ref, col) * LOG2E


def _bias_strips(tab, tq, dmin, width, rows=64):
    assert width >= pl.cdiv(rows + 2 * FAR_DIST, LANES) * LANES + LANES
    return pl.pallas_call(
        functools.partial(_strip_kernel, dmin=dmin, rows=rows),
        grid=(C_HEADS, tq // rows),
        in_specs=[pl.BlockSpec(memory_space=pltpu.SMEM)],
        out_specs=pl.BlockSpec((None, rows, width), lambda h, r: (h, r, 0)),
        out_shape=jax.ShapeDtypeStruct((C_HEADS, tq, width), F32),
        compiler_params=_params("parallel", "parallel"),
        name="bias_strips",
    )(tab)


def _diff_kernel(strip_ref, dl_ref, g_ref, q_ref, k_ref, v_ref, gate_ref, o_ref, v2_ref, *,
                 lam_init, S, tq, tk, nsub, unroll, dmin, dmax):
    i = pl.program_id(1)
    nk = S // tk
    tks = tk // nsub

    @pl.when(i == 0)
    def _():
        v2_ref[:, :C_V_DIM] = v_ref[...]
        v2_ref[:, C_V_DIM:] = jnp.ones((S, C_V_DIM), BF16)

    q = q_ref[...]
    lane = lax.broadcasted_iota(jnp.int32, q.shape, 1)
    zero = jnp.zeros_like(q)
    qs = (jnp.where(lane < C_QK_DIM, q, zero), jnp.where(lane >= C_QK_DIM, q, zero))

    def logits(j, u):
        k0 = pl.multiple_of(j * tk, tk)
        b0 = pl.multiple_of(jnp.clip(j * tk - i * tq, dmin, dmax) - dmin, LANES)
        kt = k_ref[pl.ds(k0 + u * tks, tks), :]
        b = strip_ref[:, pl.ds(b0 + u * tks, tks)]
        return [lax.dot_general(qc, kt, (((1,), (1,)), ((), ())), preferred_element_type=F32) + b for qc in qs]

    def values(j, u):
        return v2_ref[pl.ds(pl.multiple_of(j * tk, tk) + u * tks, tks), :]

    def key_tiles(first, carry):
        steps = [(first * unroll + jj, u) for jj in range(unroll) for u in range(nsub)]
        ahead = logits(*steps[0])
        for idx, (j, u) in enumerate(steps):
            ss, ahead = ahead, (logits(*steps[idx + 1]) if idx + 1 < len(steps) else None)
            ms = [jnp.maximum(m, jnp.max(s, axis=-1, keepdims=True)) for s, (m, _) in zip(ss, carry)]
            ps = [jnp.exp2((s - m_new).astype(BF16)) for s, m_new in zip(ss, ms)]
            pvs = [jnp.dot(p, values(j, u), preferred_element_type=F32) for p in ps]
            carry = tuple((m_new, jnp.exp2(m - m_new) * acc + pv) for m_new, pv, (m, acc) in zip(ms, pvs, carry))
        return carry

    start = (jnp.full((tq, 1), NEG, F32), jnp.zeros((tq, 2 * C_V_DIM), F32))
    if nk == unroll:
        carry = key_tiles(0, (start, start))
    else:
        carry = lax.fori_loop(0, nk // unroll, key_tiles, (start, start))
    on = [acc[:, :C_V_DIM] / acc[:, C_V_DIM:] for _, acc in carry]
    dl = dl_ref[...]
    lam = (jnp.exp(jnp.sum(dl[0:1] * dl[1:2], axis=-1, keepdims=True))
           - jnp.exp(jnp.sum(dl[2:3] * dl[3:4], axis=-1, keepdims=True)) + lam_init)
    o = on[0] - lam * on[1]
    y = o * lax.rsqrt(jnp.mean(o * o, axis=-1, keepdims=True) + NORM_EPS) * g_ref[...]
    o_ref[...] = (y * (1.0 - lam_init) * _silu(gate_ref[...])).astype(o_ref.dtype)


C_TQ, C_TK, C_NSUB = 512, 2048, 4


def _strip_geometry(S):
    tq, tk = min(C_TQ, S), min(C_TK, S)
    dmax = pl.cdiv(FAR_DIST + tq - 1, LANES) * LANES
    dmin = -pl.cdiv(FAR_DIST + tk - 1, LANES) * LANES
    return tq, tk, dmin, dmax, dmax - dmin + tk


def _diff_attention(strips, cqkv, diff_lam, diff_g, layer, gate_src, gate_col):
    _, _, S, hd = cqkv.shape
    tq, tk, dmin, dmax, width = _strip_geometry(S)
    nsub = C_NSUB
    assert S % tk == 0 and S % tq == 0 and strips.shape == (C_HEADS, tq, width)
    lam_init = 0.8 - 0.6 * math.exp(-0.3 * layer)
    gcb = gate_col // hd
    return pl.pallas_call(
        functools.partial(_diff_kernel, lam_init=lam_init, S=S, tq=tq, tk=tk, nsub=nsub,
                          unroll=math.gcd(S // tk, 4), dmin=dmin, dmax=dmax),
        grid=(C_HEADS, S // tq),
        in_specs=[pl.BlockSpec((None, tq, width), lambda h, i: (h, 0, 0)),
                  pl.BlockSpec((None, 4, C_QK_DIM), lambda h, i: (layer, 0, 0)),
                  pl.BlockSpec((None, 1, C_V_DIM), lambda h, i: (layer, 0, 0)),
                  pl.BlockSpec((None, None, tq, hd), lambda h, i: (h, 0, i, 0)),
                  pl.BlockSpec((None, None, S, hd), lambda h, i: (C_HEADS + h, 0, 0, 0)),
                  pl.BlockSpec((None, None, S, hd), lambda h, i: (2 * C_HEADS + h, 0, 0, 0)),
                  pl.BlockSpec((tq, hd), lambda h, i: (i, gcb + h))],
        out_specs=pl.BlockSpec((tq, hd), lambda h, i: (i, h)),
        out_shape=jax.ShapeDtypeStruct((S, C_HEADS * hd), BF16),
        scratch_shapes=[pltpu.VMEM((S, 2 * C_V_DIM), BF16)],
        compiler_params=_params("arbitrary", "arbitrary"),
        name="diff_attn",
    )(strips, diff_lam, diff_g.reshape(diff_g.shape[0], 1, C_V_DIM), cqkv, cqkv, cqkv, gate_src)


def _short_conv_kernel(b_ref, w_ref, o_ref):
    b = b_ref[...]
    S = b.shape[0]
    row = lax.broadcasted_iota(jnp.int32, b.shape, 0)
    prev = jnp.where(row == 0, 0.0, pltpu.roll(b, 1, 0))
    nxt = jnp.where(row == S - 1, 0.0, pltpu.roll(b, S - 1, 0))
    w = w_ref[...]
    o_ref[...] = w[0:1] * prev + w[1:2] * b + w[2:3] * nxt


def _short_conv(p1, hy_conv, layer, col0):
    S = p1.shape[0]
    nct = BRANCH_W // LANES
    cb = col0 // LANES
    return pl.pallas_call(
        _short_conv_kernel,
        grid=(3 * nct,),
        in_specs=[pl.BlockSpec((S, LANES), lambda c: (0, cb + c)),
                  pl.BlockSpec((None, 3, LANES), lambda c: (layer, 0, c))],
        out_specs=pl.BlockSpec((None, S, LANES), lambda c: (c // nct, 0, c % nct)),
        out_shape=jax.ShapeDtypeStruct((3, S, BRANCH_W), F32),
        compiler_params=_params("parallel"),
        name="short_conv",
    )(p1, hy_conv)


def _filter_kernel(w1t_ref, w1c_ref, w1s_ref, b1_ref, fr_ref, w2_ref, b2_ref, w3a_ref, w3b_ref, w3a0_ref, w3b0_ref,
                   skip_ref, ad_ref, o_ref, *, L, tr):
    j = pl.program_id(0)
    N = 2 * L

    def hidden(pos):
        t = pos * (1.0 / (L - 1))
        w = pos * (2.0 * math.pi / L)
        fb = (1e-4 + lax.broadcasted_iota(jnp.int32, (HY_BANDS, 1), 0).astype(F32)
              * ((HY_BANDS - 1 - 1e-4) / (HY_BANDS - 1)))
        ang = fb * w
        pre = (w1t_ref[...] * t
               + jnp.dot(w1c_ref[...], jnp.cos(ang), precision=HIGHEST, preferred_element_type=F32)
               - jnp.dot(w1s_ref[...], jnp.sin(ang), precision=HIGHEST, preferred_element_type=F32)
               + b1_ref[...])
        fr = fr_ref[...]
        hid = jnp.sin(fr[:, 0:1] * pre)
        hid = jnp.sin(fr[:, 1:2] * (jnp.dot(w2_ref[...], hid, precision=HIGHEST, preferred_element_type=F32)
                                    + b2_ref[...]))
        return hid.T

    def position(shape, dim):
        row = j * tr + lax.broadcasted_iota(jnp.int32, shape, dim)
        return row, jnp.where(row < L, row, N - row).astype(F32)

    _, pos_lane = position((1, tr), 1)
    row, pos_sub = position((tr, 1), 0)
    hid = hidden(pos_lane)
    decay = jnp.exp(-(pos_sub * (1.0 / (L - 1))) * ad_ref[...])
    for o, w3_ref in enumerate((w3a_ref, w3b_ref)):
        g = _dot3_both(hid, w3_ref[...]) * decay
        o_ref[o] = jnp.where(row == L, 0.0, g)

    @pl.when(j == 0)
    def _():
        hid0 = hidden(jnp.zeros((1, LANES), F32))[0:SUBLANES]
        r8 = lax.broadcasted_iota(jnp.int32, (SUBLANES, 1), 0)
        for o, w3_ref in enumerate((w3a0_ref, w3b0_ref)):
            hb0 = _dot3_both(hid0, w3_ref[...])
            o_ref[o, 0:SUBLANES, :] = o_ref[o, 0:SUBLANES, :] + jnp.where(r8 == 0, hb0 + skip_ref[o], 0.0)


def _hyena_filters(L, hy_w1, hy_b1, hy_freq, hy_w2, hy_b2, hy_w3, hy_skip, layer, tr=512):
    assert HY_ORDER == 2
    N = 2 * L
    W = BRANCH_W
    nb = HY_BANDS
    w1 = hy_w1[layer].T
    max_decay = math.log(HY_TARGET) / HY_FAST_PCT
    min_decay = math.log(HY_TARGET) / HY_SLOW_PCT
    absdelta = jnp.abs(jnp.linspace(min_decay, max_decay, W, dtype=F32)).reshape(1, W)
    full = lambda shape: pl.BlockSpec(shape, lambda j: (0,) * len(shape))
    nfwd = L // tr
    w3_spec = lambda o, bwd: pl.BlockSpec(
        (None, HY_FFN, W), lambda j: (layer, 0, 2 * o + ((j >= nfwd).astype(jnp.int32) if bwd is None else bwd)))
    return pl.pallas_call(
        functools.partial(_filter_kernel, L=L, tr=tr),
        grid=(N // tr,),
        in_specs=[full((HY_FFN, 1)), full((HY_FFN, nb)), full((HY_FFN, nb)), full((HY_FFN, 1)), full((HY_FFN, 2)),
                  full((HY_FFN, HY_FFN)), full((HY_FFN, 1)),
                  w3_spec(0, None), w3_spec(1, None), w3_spec(0, 1), w3_spec(1, 1),
                  pl.BlockSpec((None, HY_ORDER, 1, W), lambda j: (layer, 0, 0, 0)),
                  full((1, W))],
        out_specs=pl.BlockSpec((HY_ORDER, tr, W), lambda j: (0, j, 0)),
        out_shape=jax.ShapeDtypeStruct((HY_ORDER, N, W), F32),
        compiler_params=_params("arbitrary"),
        name="hyena_filter",
    )(w1[:, 0:1], w1[:, 1:1 + nb], w1[:, 1 + nb:], hy_b1[layer].reshape(HY_FFN, 1), hy_freq[layer].T,
      hy_w2[layer].T, hy_b2[layer].reshape(HY_FFN, 1), hy_w3, hy_w3, hy_w3, hy_w3,
      hy_skip.reshape(hy_skip.shape[0], HY_ORDER, 1, W), absdelta)


def _outer_dft_kernel(f_ref, x_ref, *rest, gated):
    gate_ref = rest[0] if gated else None
    o_ref, xs_ref, ys_ref = rest[-3:]
    K, sub, tw = x_ref.shape
    M = o_ref.shape[0]
    nch = tw // LANES
    for c in range(nch):
        xs_ref[c] = x_ref[:, :, c * LANES:(c + 1) * LANES].reshape(K * sub, LANES)
    f3 = f_ref[...]
    for s in range(sub):
        x = jnp.concatenate([xs_ref[c, pl.ds(s, K, stride=sub), :] for c in range(nch)], axis=1)
        y = _dot3(f3, x)
        for c in range(nch):
            ys_ref[c, pl.ds(s, M, stride=sub), :] = y[:, c * LANES:(c + 1) * LANES]
    for c in range(nch):
        y = ys_ref[c].reshape(M, sub, LANES)
        if gated:
            y = y * gate_ref[:, :, c * LANES:(c + 1) * LANES]
        o_ref[:, :, c * LANES:(c + 1) * LANES] = y


def _outer_dft(f3, x, xb, gate=None, gb=0, tw=BRANCH_W):
    _, K, N2, W = x.shape
    M = f3.shape[0]
    B = x.shape[0] if xb is None else 1
    x_idx = (lambda b: b) if xb is None else (lambda b: xb)
    blk = lambda rows, idx: pl.BlockSpec((None, rows, SUBLANES, tw), lambda b, t, c: (idx(b), 0, t, c))
    in_specs = [pl.BlockSpec((M, 3 * K), lambda b, t, c: (0, 0)), blk(K, x_idx)]
    args = [f3, x]
    if gate is not None:
        in_specs.append(blk(M, lambda b: gb))
        args.append(gate)
    return pl.pallas_call(
        functools.partial(_outer_dft_kernel, gated=gate is not None),
        grid=(B, N2 // SUBLANES, W // tw),
        in_specs=in_specs,
        out_specs=blk(M, lambda b: b),
        out_shape=jax.ShapeDtypeStruct((B, M, N2, W), F32),
        scratch_shapes=[pltpu.VMEM((tw // LANES, K * SUBLANES, LANES), F32),
                        pltpu.VMEM((tw // LANES, M * SUBLANES, LANES), F32)],
        compiler_params=_params("parallel", "parallel", "parallel"),
        name="dft_outer",
    )(*args)


def _conv_mid_kernel(gf_ref, gi_ref, a_ref, f_ref, o_ref):
    half = a_ref.shape[0] // 2
    gf3 = gf_ref[...]
    x = _dot3(gf3, a_ref[...])
    h = _dot3(gf3, f_ref[...])
    xr, xi = x[:half], x[half:]
    hr, hi = h[:half], h[half:]
    y = jnp.concatenate([xr * hr - xi * hi, xr * hi + xi * hr], axis=0)
    o_ref[...] = _dot3(gi_ref[...], y)


def _conv_mid(gf3, gi3, a, hspec, order):
    N1, R, W = a.shape
    return pl.pallas_call(
        _conv_mid_kernel,
        grid=(N1,),
        in_specs=[pl.BlockSpec((None, R, 3 * R), lambda k: (k, 0, 0)),
                  pl.BlockSpec((None, R, 3 * R), lambda k: (k, 0, 0)),
                  pl.BlockSpec((None, R, W), lambda k: (k, 0, 0)),
                  pl.BlockSpec((None, None, R, W), lambda k: (order, k, 0, 0))],
        out_specs=pl.BlockSpec((None, R, W), lambda k: (k, 0, 0)),
        out_shape=jax.ShapeDtypeStruct(a.shape, F32),
        compiler_params=_params("parallel"),
        name="conv_mid",
    )(gf3, gi3, a, hspec)


def _dft_tables(L):
    N = 2 * L
    N2 = LANES
    N1 = N // N2
    NS = pl.cdiv(N1 // 2 + 1, SUBLANES) * SUBLANES
    two_pi = 2.0 * math.pi
    ks = jnp.arange(NS, dtype=jnp.int32)
    t1 = jnp.arange(N1, dtype=jnp.int32)
    live = (ks <= N1 // 2).astype(F32)
    ang = ((ks[:, None] * t1[None, :]) % N1).astype(F32) * (two_pi / N1)
    cs = jnp.stack([jnp.cos(ang), -jnp.sin(ang)], axis=1) * live[:, None, None]
    fa = cs.reshape(2 * NS, N1)
    weight = jnp.where((ks == 0) | (ks == N1 // 2), 1.0, 2.0) / N
    fi = jnp.transpose(cs * weight[:, None, None], (2, 0, 1)).reshape(N1, 2 * NS)[:N1 // 2]
    t2 = jnp.arange(N2, dtype=jnp.int32)
    k = ks[:, None, None] + N1 * t2[None, :, None]
    ph = ((k * t2[None, None, :]) % N).astype(F32) * (two_pi / N)
    gr, gim = jnp.cos(ph), -jnp.sin(ph)
    gf = jnp.concatenate([jnp.concatenate([gr, -gim], axis=2), jnp.concatenate([gim, gr], axis=2)], axis=1)
    gi = jnp.swapaxes(gf, 1, 2)
    return {"fa": _const3(fa), "fh": _const3(fa[:, :N1 // 2]), "fi": _const3(fi), "gf": _const3(gf),
            "gi": _const3(gi), "N1": N1, "N2": N2, "NS": NS}


def _long_conv(z, zi, gate, gi_, hspec, order, tb):
    _, S, W = z.shape
    N1, N2, NS = tb["N1"], tb["N2"], tb["NS"]
    split = lambda t: t.reshape(t.shape[0], N1 // 2, N2, W)
    a = _outer_dft(tb["fh"], split(z), zi)
    b = _conv_mid(tb["gf"], tb["gi"], a.reshape(NS, 2 * N2, W), hspec, order)
    y = _outer_dft(tb["fi"], b.reshape(1, 2 * NS, N2, W), 0, split(gate), gi_)
    return y.reshape(1, S, W)


def _merge_kernel(a_ref, zb_ref, bg_ref, c_ref, w_ref, ma_ref, mb_ref, mc_ref, bias_ref, o_ref):
    b = (zb_ref[...] * _silu(bg_ref[...])).astype(BF16)
    bias = bias_ref[...]
    y = None
    for n, (br, m_ref) in enumerate(((a_ref[...], ma_ref), (b, mb_ref), (c_ref[...], mc_ref))):
        pb = jnp.dot(br, w_ref[n], preferred_element_type=F32)
        gate = _sigmoid(m_ref[...] + bias[n:n + 1])
        y = gate * pb if y is None else y + gate * pb
    o_ref[...] = y.astype(o_ref.dtype)


def _merge(a_out, zb, p1, c_out, mg, w_proj_bf, merge_b, layer, tm=256):
    S = a_out.shape[0]
    W, D = BRANCH_W, D_MODEL
    bgb = (P1_COLS - W) // W
    branch = pl.BlockSpec((tm, W), lambda i: (i, 0))
    mspec = lambda n: pl.BlockSpec((tm, D), lambda i: (i, n))
    return pl.pallas_call(
        _merge_kernel,
        grid=(S // tm,),
        in_specs=[branch, pl.BlockSpec((None, tm, W), lambda i: (0, i, 0)),
                  pl.BlockSpec((tm, W), lambda i: (i, bgb)), branch,
                  pl.BlockSpec((None, N_BRANCH, W, D), lambda i: (layer, 0, 0, 0)),
                  mspec(0), mspec(1), mspec(2),
                  pl.BlockSpec((None, N_BRANCH, D), lambda i: (layer, 0, 0))],
        out_specs=pl.BlockSpec((tm, D), lambda i: (i, 0)),
        out_shape=jax.ShapeDtypeStruct((S, D), BF16),
        compiler_params=_params("parallel"),
        name="merge",
    )(a_out, zb, p1, c_out, w_proj_bf, mg, mg, mg, merge_b)


def _out_kernel(x_ref, y_ref, w_ref, g_ref, xo_ref, ho_ref):
    x = x_ref[...] + jnp.dot(y_ref[...], w_ref[...], preferred_element_type=F32)
    xo_ref[...] = x
    ms = jnp.mean(x * x, axis=-1, keepdims=True)
    ho_ref[...] = (x * lax.rsqrt(ms + NORM_EPS) * g_ref[...]).astype(ho_ref.dtype)


def _out_proj(x, y, w_out_bf, layer, g, h_dtype, tm=512):
    S, D = x.shape
    row = pl.BlockSpec((tm, D), lambda i: (i, 0))
    return pl.pallas_call(
        _out_kernel,
        grid=(S // tm,),
        in_specs=[row, row, pl.BlockSpec((None, D, D), lambda i: (layer, 0, 0)), pl.BlockSpec((1, D), lambda i: (0, 0))],
        out_specs=(row, row),
        out_shape=(jax.ShapeDtypeStruct((S, D), F32), jax.ShapeDtypeStruct((S, D), h_dtype)),
        compiler_params=_params("parallel"),
        name="out_proj",
    )(x, y, w_out_bf, g.reshape(1, D))


def _encode(xb, norm_g, final_g, w_in, merge_b, rel_bias, hy_conv, hy_w1, hy_b1, hy_freq, hy_w2, hy_b2, hy_w3,
            hy_skip, diff_lam, diff_g, w_proj_bf, w_out_bf, tb):
    S = xb.shape[0]
    depth = w_in.shape[0]
    N1, N2 = tb["N1"], tb["N2"]
    h = _rmsnorm(xb, norm_g[0], BF16)
    for l in range(depth):
        qkv = [_in_proj(h, w_in, l, COL_A + g * A_GROUP_COLS, A_GROUP_COLS, dil=DIL_PAIRS[g][1]) for g in range(3)]
        p1 = _in_proj(h, w_in, l, COL_P1, P1_COLS)
        cqkv = _in_proj(h, w_in, l, COL_C, C_QKV_COLS, dil=1, scale0=C_Q_SCALE)
        cg = _in_proj(h, w_in, l, COL_P2, BRANCH_W)
        mg = _in_proj(h, w_in, l, COL_P2 + BRANCH_W, N_BRANCH * D_MODEL)

        others = [_dilated_group(rel_bias, qkv[g], g, None, None) for g in (1, 2)]
        a_out = _dilated_group(rel_bias, qkv[0], 0, others, p1, tq=A_CHAINS * A_SUB)

        u = _short_conv(p1, hy_conv, l, BRANCH_W)
        filt = _hyena_filters(S, hy_w1, hy_b1, hy_freq, hy_w2, hy_b2, hy_w3, hy_skip, l)
        fa_out = _outer_dft(tb["fa"], filt.reshape(HY_ORDER, N1, N2, BRANCH_W), None)
        hspec = fa_out.reshape(HY_ORDER, tb["NS"], 2 * N2, BRANCH_W)
        z = _long_conv(u, 0, u, 1, hspec, 0, tb)
        zb = _long_conv(z, 0, u, 2, hspec, 1, tb)

        c_out = _diff_attention(tb["strips"], cqkv, diff_lam, diff_g, l, cg, 0)

        y = _merge(a_out, zb, p1, c_out, mg, w_proj_bf, merge_b, l)
        if l + 1 < depth:
            xb, h = _out_proj(xb, y, w_out_bf, l, norm_g[l + 1], BF16)
        else:
            _, h = _out_proj(xb, y, w_out_bf, l, final_g, F32)
    return h


def kernel(x, norm_g, final_g, w_in, merge_b, rel_bias, hy_conv, hy_w1, hy_b1, hy_freq, hy_w2, hy_b2, hy_w3,
           hy_skip, diff_lam, diff_g, w_proj, w_out):
    B, S, D = x.shape
    w_proj_bf = w_proj.astype(BF16)
    w_out_bf = w_out.astype(BF16)
    tb = _dft_tables(S)
    tq, _, dmin, _, width = _strip_geometry(S)
    tb["strips"] = _bias_strips(rel_bias, tq, dmin, width)
    outs = [_encode(x[b], norm_g, final_g, w_in, merge_b, rel_bias, hy_conv, hy_w1, hy_b1, hy_freq, hy_w2, hy_b2,
                    hy_w3, hy_skip, diff_lam, diff_g, w_proj_bf, w_out_bf, tb) for b in range(B)]
    return outs[0].reshape(1, S, D) if B == 1 else jnp.stack(outs)
```

```python
import functools
import math

import jax
import jax.numpy as jnp
from jax import lax
from jax.experimental import pallas as pl
from jax.experimental.pallas import tpu as pltpu

F32 = jnp.float32
BF16 = jnp.bfloat16

LANES = 128
SUBLANES = 8
D_MODEL = 2048
BRANCH_W = 1024
N_BRANCH = 3
DIL_PAIRS = ((128, 1), (512, 4), (2048, 16))
A_HEADS = 8
A_HEAD_DIM = 128
REACH = 64
A_SUB = 128
A_KW = A_SUB + 2 * REACH
A_CHAINS = 8
C_HEADS = 8
C_QK_DIM = 64
C_V_DIM = 128
HY_BANDS = 16
HY_FFN = 64
HY_ORDER = 2
HY_TARGET = 1e-2
HY_FAST_PCT = 0.3
HY_SLOW_PCT = 1.5
N_BUCKETS = 32
REL_MAX_DIST = 1024
NORM_EPS = 1e-6
NEG = -1e30

A_GROUP_COLS = 3 * A_HEADS * A_HEAD_DIM
A_QKV_COLS = 3 * A_GROUP_COLS
P1_COLS = BRANCH_W + 3 * BRANCH_W + BRANCH_W
C_QKV_COLS = 3 * C_HEADS * C_V_DIM
P2_COLS = BRANCH_W + N_BRANCH * D_MODEL
COL_A, COL_P1, COL_C, COL_P2 = 0, A_QKV_COLS, A_QKV_COLS + P1_COLS, A_QKV_COLS + P1_COLS + C_QKV_COLS

VMEM_LIMIT = 56 * 1024 * 1024
HIGHEST = lax.Precision.HIGHEST
LOG2E = math.log2(math.e)
C_Q_SCALE = LOG2E / math.sqrt(C_QK_DIM)

def _bucket_thresholds():
    nb = N_BUCKETS // 2
    max_exact = nb // 2
    thr = []
    for k in range(1, nb - max_exact):
        n = max_exact
        while int(math.log(n / max_exact) / math.log(REL_MAX_DIST / max_exact) * (nb - max_exact)) < k:
            n += 1
        thr.append(n)
    return tuple(thr)


BUCKET_THR = _bucket_thresholds()
FAR_DIST = BUCKET_THR[-1]


def _params(*sem):
    return pltpu.CompilerParams(dimension_semantics=sem, vmem_limit_bytes=VMEM_LIMIT)


def _rel_bias(rel, tab_ref, col):
    nb = N_BUCKETS // 2
    n = jnp.abs(rel)
    large = jnp.full(rel.shape, nb // 2, jnp.int32)
    for thr in BUCKET_THR:
        large = large + (n >= thr).astype(jnp.int32)
    bucket = jnp.where(n < nb // 2, n, large) + jnp.where(rel > 0, nb, 0)
    out = jnp.full(rel.shape, tab_ref[0, col], F32)
    for j in range(1, N_BUCKETS):
        out = jnp.where(bucket == j, tab_ref[j, col], out)
    return out


def _sigmoid(x):
    return 0.5 * jnp.tanh(0.5 * x) + 0.5


def _silu(x):
    return x * _sigmoid(x)


def _split_bf16(x):
    hi = x.astype(BF16)
    return hi, (x - hi.astype(F32)).astype(BF16)


def _const3(a):
    hi, lo = _split_bf16(a)
    return jnp.concatenate([hi, hi, lo], axis=-1)


def _dot3(a3, x):
    hi, lo = _split_bf16(x)
    return jnp.dot(a3, jnp.concatenate([hi, lo, hi], axis=0), preferred_element_type=F32)


def _dot3_both(a, x):
    return _dot3(_const3(a), x)


def _rmsnorm_kernel(x_ref, g_ref, o_ref):
    x = x_ref[...]
    ms = jnp.mean(x * x, axis=-1, keepdims=True)
    o_ref[...] = (x * lax.rsqrt(ms + NORM_EPS) * g_ref[...]).astype(o_ref.dtype)


def _rmsnorm(x, g, out_dtype, tm=512):
    S, D = x.shape
    return pl.pallas_call(
        _rmsnorm_kernel,
        grid=(S // tm,),
        in_specs=[pl.BlockSpec((tm, D), lambda i: (i, 0)), pl.BlockSpec((1, D), lambda i: (0, 0))],
        out_specs=pl.BlockSpec((tm, D), lambda i: (i, 0)),
        out_shape=jax.ShapeDtypeStruct((S, D), out_dtype),
        compiler_params=_params("parallel"),
        name="rmsnorm",
    )(x, g.reshape(1, D))


def _proj_kernel(h_ref, w_ref, o_ref, wb_ref, *acc_ref, dil, scale0):
    @pl.when(pl.program_id(1) == 0)
    def _():
        wb_ref[...] = w_ref[...].astype(BF16)

    acc = jnp.dot(h_ref[...], wb_ref[...], preferred_element_type=F32)
    if scale0 is not None:
        acc = acc * jnp.where(pl.program_id(0) == 0, scale0, 1.0)
    tm, tn = acc.shape
    if dil is None:
        o_ref[...] = acc
    elif dil == 1:
        for c in range(tn // LANES):
            o_ref[c, 0] = acc[:, c * LANES:(c + 1) * LANES].astype(o_ref.dtype)
    else:
        for c in range(tn // LANES):
            acc_ref[0][c] = acc[:, c * LANES:(c + 1) * LANES]
            for r in range(dil):
                o_ref[c, r] = acc_ref[0][c, pl.ds(r, tm // dil, stride=dil), :].astype(o_ref.dtype)


def _in_proj(h, w_in, layer, col0, ncols, dil=None, scale0=None, tm=1024, tn=1024):
    S, D = h.shape
    assert col0 % tn == 0 and ncols % tn == 0
    jb = col0 // tn
    scratch = [pltpu.VMEM((D, tn), BF16)]
    if dil is None:
        out_shape = jax.ShapeDtypeStruct((S, ncols), F32)
        out_spec = pl.BlockSpec((tm, tn), lambda j, i: (i, j))
    else:
        out_shape = jax.ShapeDtypeStruct((ncols // LANES, dil, S // dil, LANES), BF16)
        out_spec = pl.BlockSpec((tn // LANES, dil, tm // dil, LANES), lambda j, i: (j, 0, i, 0))
        if dil > 1:
            scratch.append(pltpu.VMEM((tn // LANES, tm, LANES), F32))
    return pl.pallas_call(
        functools.partial(_proj_kernel, dil=dil, scale0=scale0),
        grid=(ncols // tn, S // tm),
        in_specs=[pl.BlockSpec((tm, D), lambda j, i: (i, 0)),
                  pl.BlockSpec((None, D, tn), lambda j, i: (layer, 0, jb + j))],
        out_specs=out_spec,
        out_shape=out_shape,
        scratch_shapes=scratch,
        compiler_params=_params("parallel", "arbitrary"),
        name="in_proj",
    )(h, w_in)


def _dil_kernel(*refs, dil, col0, others, n, tq):
    it = iter(refs)
    tab_ref, q_ref, k_ref, v_ref = next(it), next(it), next(it), next(it)
    other_refs = [(next(it), next(it)) for _ in others]
    gate_ref = next(it) if others else None
    o_ref = next(it)
    lse_ref = None if others else next(it)
    bm_ref = next(it)
    pos_refs = [(next(it), next(it)) for _ in others]

    h, r, i = pl.program_id(0), pl.program_id(1), pl.program_id(2)
    deltas = (-REACH, 0, -2 * REACH)

    @pl.when((r == 0) & (i == 0))
    def _():
        t = lax.broadcasted_iota(jnp.int32, (A_SUB, A_KW), 0)
        u = lax.broadcasted_iota(jnp.int32, (A_SUB, A_KW), 1)
        for idx, delta in enumerate(deltas):
            rel = u + delta - t
            bias = _rel_bias(rel * dil, tab_ref, col0 + h)
            bm_ref[idx] = jnp.where(jnp.abs(rel) <= REACH, bias, NEG)

    for (src_o, src_l), (dst_o, dst_l), d in zip(other_refs, pos_refs, others):
        for rr in range(d):
            dst_o[pl.ds(rr, tq // d, stride=d), :] = src_o[rr]
            dst_l[pl.ds(rr, tq // d, stride=d), :] = src_l[rr]

    rb = q_ref.shape[0]
    subs = [(rr, sb) for rr in range(rb) for sb in range(tq // A_SUB)]
    row_sl = [slice(sb * A_SUB, (sb + 1) * A_SUB) for _, sb in subs]
    m0s = [i * tq + sb * A_SUB for _, sb in subs]
    kss = [pl.multiple_of(jnp.clip(m0 - REACH, 0, n - A_KW), REACH) for m0 in m0s]
    sels = [jnp.where(m0 == 0, 1, jnp.where(m0 == n - A_SUB, 2, 0)) for m0 in m0s]
    ss = [lax.dot_general(q_ref[rr, rows, :], k_ref[rr, pl.ds(ks, A_KW), :], (((1,), (1,)), ((), ())),
                          preferred_element_type=F32) for (rr, _), rows, ks in zip(subs, row_sl, kss)]
    ss = [s * (1.0 / math.sqrt(A_HEAD_DIM)) + bm_ref[sel] for s, sel in zip(ss, sels)]
    ms = [jnp.max(s, axis=-1, keepdims=True) for s in ss]
    ps = [jnp.exp(s - m) for s, m in zip(ss, ms)]
    ls = [jnp.sum(p, axis=-1, keepdims=True) for p in ps]
    pvs = [jnp.dot(p.astype(BF16), v_ref[rr, pl.ds(ks, A_KW), :], preferred_element_type=F32)
           for (rr, _), p, ks in zip(subs, ps, kss)]
    for (rr, _), rows, m, l, pv in zip(subs, row_sl, ms, ls, pvs):
        o = pv / l
        lse = jnp.broadcast_to(m + jnp.log(l), o.shape)
        if others:
            lses = [lse] + [dst_l[rows, :] for _, dst_l in pos_refs]
            outs = [o] + [dst_o[rows, :] for dst_o, _ in pos_refs]
            mx = functools.reduce(jnp.maximum, lses)
            ws = [jnp.exp(x - mx) for x in lses]
            o = sum(w * x for w, x in zip(ws, outs)) / sum(ws)
            o_ref[rows, :] = (o * _silu(gate_ref[rows, :])).astype(o_ref.dtype)
        else:
            o_ref[rr, rows, :] = o
            lse_ref[rr, rows, :] = lse


def _dilated_group(tab, qkv, g, others, gate_src, tq=512):
    win, dil = DIL_PAIRS[g]
    assert win // (2 * dil) == REACH
    _, _, n, hd = qkv.shape
    tq = min(tq, n)
    rb = min(dil, A_CHAINS * A_SUB // tq)
    assert qkv.shape[1] == dil and n % tq == 0 and n >= A_KW and dil % rb == 0
    last = others is not None
    qblk = pl.BlockSpec((None, rb, tq, hd), lambda h, r, i: (h, r, i, 0))
    in_specs = [pl.BlockSpec(memory_space=pltpu.SMEM), qblk,
                pl.BlockSpec((None, rb, n, hd), lambda h, r, i: (A_HEADS + h, r, 0, 0)),
                pl.BlockSpec((None, rb, n, hd), lambda h, r, i: (2 * A_HEADS + h, r, 0, 0))]
    args = [tab, qkv, qkv, qkv]
    scratch = [pltpu.VMEM((3, A_SUB, A_KW), F32)]
    other_dils = ()
    if last:
        assert dil == 1
        other_dils = tuple(o.shape[1] for o, _ in others)
        for (o, lse), d in zip(others, other_dils):
            spec = pl.BlockSpec((None, d, tq // d, hd), lambda h, r, i: (h, 0, i, 0))
            in_specs += [spec, spec]
            args += [o, lse]
            scratch += [pltpu.VMEM((tq, hd), F32), pltpu.VMEM((tq, hd), F32)]
        in_specs.append(pl.BlockSpec((tq, hd), lambda h, r, i: (i, h)))
        args.append(gate_src)
        out_shape = jax.ShapeDtypeStruct((n, A_HEADS * hd), BF16)
        out_specs = pl.BlockSpec((tq, hd), lambda h, r, i: (i, h))
    else:
        out_shape = (jax.ShapeDtypeStruct((A_HEADS, dil, n, hd), F32),) * 2
        out_specs = (qblk, qblk)
    return pl.pallas_call(
        functools.partial(_dil_kernel, dil=dil, col0=g * A_HEADS, others=other_dils, n=n, tq=tq),
        grid=(A_HEADS, dil // rb, n // tq),
        in_specs=in_specs,
        out_specs=out_specs,
        out_shape=out_shape,
        scratch_shapes=scratch,
        compiler_params=_params("arbitrary", "arbitrary", "arbitrary"),
        name=f"dilated_attn_g{g}",
    )(*args)


def _strip_kernel(tab_ref, o_ref, *, dmin, rows):
    h, r = pl.program_id(0), pl.program_id(1)
    width = o_ref.shape[1]
    col = 3 * A_HEADS + h
    far_lo = tab_ref[N_BUCKETS // 2 - 1, col] * LOG2E
    far_hi = tab_ref[N_BUCKETS - 1, col] * LOG2E
    r0 = r * rows
    band = pl.cdiv(rows + 2 * FAR_DIST, LANES) * LANES + LANES
    lo = pl.multiple_of(jnp.clip((r0 - dmin - FAR_DIST) // LANES * LANES, 0, width - band), LANES)
    c_all = lax.broadcasted_iota(jnp.int32, (rows, width), 1)
    t_all = r0 + lax.broadcasted_iota(jnp.int32, (rows, width), 0)
    o_ref[...] = jnp.where(c_all - t_all + dmin < 0, far_lo, far_hi)
    t = r0 + lax.broadcasted_iota(jnp.int32, (rows, band), 0)
    c = lo + lax.broadcasted_iota(jnp.int32, (rows, band), 1)
    o_ref[:, pl.ds(lo, band)] = _rel_bias(c - t + dmin, tab_ref, col) * LOG2E


def _bias_strips(tab, tq, dmin, width, rows=64):
    assert width >= pl.cdiv(rows + 2 * FAR_DIST, LANES) * LANES + LANES
    return pl.pallas_call(
        functools.partial(_strip_kernel, dmin=dmin, rows=rows),
        grid=(C_HEADS, tq // rows),
        in_specs=[pl.BlockSpec(memory_space=pltpu.SMEM)],
        out_specs=pl.BlockSpec((None, rows, width), lambda h, r: (h, r, 0)),
        out_shape=jax.ShapeDtypeStruct((C_HEADS, tq, width), F32),
        compiler_params=_params("parallel", "parallel"),
        name="bias_strips",
    )(tab)


def _diff_kernel(strip_ref, dl_ref, g_ref, q_ref, k_ref, v_ref, gate_ref, o_ref, v2_ref, *,
                 lam_init, S, tq, tk, nsub, unroll, dmin, dmax):
    i = pl.program_id(1)
    nk = S // tk
    tks = tk // nsub

    @pl.when(i == 0)
    def _():
        v2_ref[:, :C_V_DIM] = v_ref[...]
        v2_ref[:, C_V_DIM:] = jnp.ones((S, C_V_DIM), BF16)

    q = q_ref[...]
    lane = lax.broadcasted_iota(jnp.int32, q.shape, 1)
    zero = jnp.zeros_like(q)
    qs = (jnp.where(lane < C_QK_DIM, q, zero), jnp.where(lane >= C_QK_DIM, q, zero))

    def logits(j, u):
        k0 = pl.multiple_of(j * tk, tk)
        b0 = pl.multiple_of(jnp.clip(j * tk - i * tq, dmin, dmax) - dmin, LANES)
        kt = k_ref[pl.ds(k0 + u * tks, tks), :]
        b = strip_ref[:, pl.ds(b0 + u * tks, tks)]
        return [lax.dot_general(qc, kt, (((1,), (1,)), ((), ())), preferred_element_type=F32) + b for qc in qs]

    def values(j, u):
        return v2_ref[pl.ds(pl.multiple_of(j * tk, tk) + u * tks, tks), :]

    def key_tiles(first, carry):
        steps = [(first * unroll + jj, u) for jj in range(unroll) for u in range(nsub)]
        ahead = logits(*steps[0])
        for idx, (j, u) in enumerate(steps):
            ss, ahead = ahead, (logits(*steps[idx + 1]) if idx + 1 < len(steps) else None)
            ms = [jnp.maximum(m, jnp.max(s, axis=-1, keepdims=True)) for s, (m, _) in zip(ss, carry)]
            ps = [jnp.exp2((s - m_new).astype(BF16)) for s, m_new in zip(ss, ms)]
            pvs = [jnp.dot(p, values(j, u), preferred_element_type=F32) for p in ps]
            carry = tuple((m_new, jnp.exp2(m - m_new) * acc + pv) for m_new, pv, (m, acc) in zip(ms, pvs, carry))
        return carry

    start = (jnp.full((tq, 1), NEG, F32), jnp.zeros((tq, 2 * C_V_DIM), F32))
    if nk == unroll:
        carry = key_tiles(0, (start, start))
    else:
        carry = lax.fori_loop(0, nk // unroll, key_tiles, (start, start))
    on = [acc[:, :C_V_DIM] / acc[:, C_V_DIM:] for _, acc in carry]
    dl = dl_ref[...]
    lam = (jnp.exp(jnp.sum(dl[0:1] * dl[1:2], axis=-1, keepdims=True))
           - jnp.exp(jnp.sum(dl[2:3] * dl[3:4], axis=-1, keepdims=True)) + lam_init)
    o = on[0] - lam * on[1]
    y = o * lax.rsqrt(jnp.mean(o * o, axis=-1, keepdims=True) + NORM_EPS) * g_ref[...]
    o_ref[...] = (y * (1.0 - lam_init) * _silu(gate_ref[...])).astype(o_ref.dtype)


C_TQ, C_TK, C_NSUB = 512, 2048, 4


def _strip_geometry(S):
    tq, tk = min(C_TQ, S), min(C_TK, S)
    dmax = pl.cdiv(FAR_DIST + tq - 1, LANES) * LANES
    dmin = -pl.cdiv(FAR_DIST + tk - 1, LANES) * LANES
    return tq, tk, dmin, dmax, dmax - dmin + tk


def _diff_attention(strips, cqkv, diff_lam, diff_g, layer, gate_src, gate_col):
    _, _, S, hd = cqkv.shape
    tq, tk, dmin, dmax, width = _strip_geometry(S)
    nsub = C_NSUB
    assert S % tk == 0 and S % tq == 0 and strips.shape == (C_HEADS, tq, width)
    lam_init = 0.8 - 0.6 * math.exp(-0.3 * layer)
    gcb = gate_col // hd
    return pl.pallas_call(
        functools.partial(_diff_kernel, lam_init=lam_init, S=S, tq=tq, tk=tk, nsub=nsub,
                          unroll=math.gcd(S // tk, 4), dmin=dmin, dmax=dmax),
        grid=(C_HEADS, S // tq),
        in_specs=[pl.BlockSpec((None, tq, width), lambda h, i: (h, 0, 0)),
                  pl.BlockSpec((None, 4, C_QK_DIM), lambda h, i: (layer, 0, 0)),
                  pl.BlockSpec((None, 1, C_V_DIM), lambda h, i: (layer, 0, 0)),
                  pl.BlockSpec((None, None, tq, hd), lambda h, i: (h, 0, i, 0)),
                  pl.BlockSpec((None, None, S, hd), lambda h, i: (C_HEADS + h, 0, 0, 0)),
                  pl.BlockSpec((None, None, S, hd), lambda h, i: (2 * C_HEADS + h, 0, 0, 0)),
                  pl.BlockSpec((tq, hd), lambda h, i: (i, gcb + h))],
        out_specs=pl.BlockSpec((tq, hd), lambda h, i: (i, h)),
        out_shape=jax.ShapeDtypeStruct((S, C_HEADS * hd), BF16),
        scratch_shapes=[pltpu.VMEM((S, 2 * C_V_DIM), BF16)],
        compiler_params=_params("arbitrary", "arbitrary"),
        name="diff_attn",
    )(strips, diff_lam, diff_g.reshape(diff_g.shape[0], 1, C_V_DIM), cqkv, cqkv, cqkv, gate_src)


def _short_conv_kernel(b_ref, w_ref, o_ref):
    b = b_ref[...]
    S = b.shape[0]
    row = lax.broadcasted_iota(jnp.int32, b.shape, 0)
    prev = jnp.where(row == 0, 0.0, pltpu.roll(b, 1, 0))
    nxt = jnp.where(row == S - 1, 0.0, pltpu.roll(b, S - 1, 0))
    w = w_ref[...]
    o_ref[...] = w[0:1] * prev + w[1:2] * b + w[2:3] * nxt


def _short_conv(p1, hy_conv, layer, col0):
    S = p1.shape[0]
    nct = BRANCH_W // LANES
    cb = col0 // LANES
    return pl.pallas_call(
        _short_conv_kernel,
        grid=(3 * nct,),
        in_specs=[pl.BlockSpec((S, LANES), lambda c: (0, cb + c)),
                  pl.BlockSpec((None, 3, LANES), lambda c: (layer, 0, c))],
        out_specs=pl.BlockSpec((None, S, LANES), lambda c: (c // nct, 0, c % nct)),
        out_shape=jax.ShapeDtypeStruct((3, S, BRANCH_W), F32),
        compiler_params=_params("parallel"),
        name="short_conv",
    )(p1, hy_conv)


def _filter_dft_kernel(w1t_ref, w1c_ref, w1s_ref, b1_ref, fr_ref, w2_ref, b2_ref, w30f_ref, w30b_ref, w31f_ref,
                       w31b_ref, skip_ref, ad_ref, fa_ref, o_ref, ys_ref, *, L):
    j = pl.program_id(0)
    N = 2 * L
    N1 = fa_ref.shape[1] // 3
    N2 = N // N1
    M = fa_ref.shape[0]
    nch = o_ref.shape[-1] // LANES

    def hidden(pos):
        t = pos * (1.0 / (L - 1))
        w = pos * (2.0 * math.pi / L)
        fb = (1e-4 + lax.broadcasted_iota(jnp.int32, (HY_BANDS, 1), 0).astype(F32)
              * ((HY_BANDS - 1 - 1e-4) / (HY_BANDS - 1)))
        ang = fb * w
        pre = (w1t_ref[...] * t
               + jnp.dot(w1c_ref[...], jnp.cos(ang), precision=HIGHEST, preferred_element_type=F32)
               - jnp.dot(w1s_ref[...], jnp.sin(ang), precision=HIGHEST, preferred_element_type=F32)
               + b1_ref[...])
        fr = fr_ref[...]
        hid = jnp.sin(fr[:, 0:1] * pre)
        hid = jnp.sin(fr[:, 1:2] * (jnp.dot(w2_ref[...], hid, precision=HIGHEST, preferred_element_type=F32)
                                    + b2_ref[...]))
        return hid.T

    def split_rows(x):
        hi, lo = _split_bf16(x)
        return jnp.concatenate([hi, lo, hi], axis=0)

    w3 = [[split_rows(ref[...]) for ref in pair] for pair in ((w30f_ref, w30b_ref), (w31f_ref, w31b_ref))]
    hid0 = _const3(jnp.broadcast_to(hidden(jnp.zeros((1, LANES), F32))[0:1], (SUBLANES, HY_FFN)))
    lag0 = [jnp.dot(hid0, w3[o][1], preferred_element_type=F32)[0:1] + skip_ref[o] for o in range(HY_ORDER)]
    fa3 = fa_ref[...]
    half = N1 // 2
    lag = lambda r: jnp.where(r < L, r, N - r).astype(F32)
    lane = lax.broadcasted_iota(jnp.int32, (1, SUBLANES * N1), 1)
    hid_all = hidden(lag((lane % N1) * N2 + j * SUBLANES + lane // N1))
    rows = [lax.broadcasted_iota(jnp.int32, (N1, 1), 0) * N2 + j * SUBLANES + s for s in range(SUBLANES)]
    hids = [(_const3(hid_all[s * N1:s * N1 + half]), _const3(hid_all[s * N1 + half:(s + 1) * N1]))
            for s in range(SUBLANES)]
    decays = [jnp.exp(-(lag(row) * (1.0 / (L - 1))) * ad_ref[...]) for row in rows]
    pairs = [(s, o) for s in range(SUBLANES) for o in range(HY_ORDER)]
    gs = [jnp.concatenate([jnp.dot(hids[s][0], w3[o][0], preferred_element_type=F32),
                           jnp.dot(hids[s][1], w3[o][1], preferred_element_type=F32)], axis=0) * decays[s]
          for s, o in pairs]
    gs = [jnp.where(rows[0] == L, 0.0, g + jnp.where(rows[0] == 0, lag0[o], 0.0)) if s == 0 else g
          for g, (s, o) in zip(gs, pairs)]
    ys = [_dot3(fa3, g) for g in gs]
    for y, (s, o) in zip(ys, pairs):
        for c in range(nch):
            ys_ref[o, c, pl.ds(s, M, stride=SUBLANES), :] = y[:, c * LANES:(c + 1) * LANES]
    for o in range(HY_ORDER):
        for c in range(nch):
            o_ref[o, :, :, c * LANES:(c + 1) * LANES] = ys_ref[o, c].reshape(M, SUBLANES, LANES)


def _filter_outer_dft(L, fa3, hy_w1, hy_b1, hy_freq, hy_w2, hy_b2, hy_w3, hy_skip, layer):
    assert HY_ORDER == 2
    N = 2 * L
    W = BRANCH_W
    nb = HY_BANDS
    M, N1 = fa3.shape[0], fa3.shape[1] // 3
    N2 = N // N1
    w1 = hy_w1[layer].T
    max_decay = math.log(HY_TARGET) / HY_FAST_PCT
    min_decay = math.log(HY_TARGET) / HY_SLOW_PCT
    absdelta = jnp.abs(jnp.linspace(min_decay, max_decay, W, dtype=F32)).reshape(1, W)
    full = lambda shape: pl.BlockSpec(shape, lambda j: (0,) * len(shape))
    w3_spec = lambda o, bwd: pl.BlockSpec((None, HY_FFN, W), lambda j: (layer, 0, 2 * o + bwd))
    return pl.pallas_call(
        functools.partial(_filter_dft_kernel, L=L),
        grid=(N2 // SUBLANES,),
        in_specs=[full((HY_FFN, 1)), full((HY_FFN, nb)), full((HY_FFN, nb)), full((HY_FFN, 1)), full((HY_FFN, 2)),
                  full((HY_FFN, HY_FFN)), full((HY_FFN, 1)),
                  w3_spec(0, 0), w3_spec(0, 1), w3_spec(1, 0), w3_spec(1, 1),
                  pl.BlockSpec((None, HY_ORDER, 1, W), lambda j: (layer, 0, 0, 0)),
                  full((1, W)), full((M, 3 * N1))],
        out_specs=pl.BlockSpec((HY_ORDER, M, SUBLANES, W), lambda j: (0, 0, j, 0)),
        out_shape=jax.ShapeDtypeStruct((HY_ORDER, M, N2, W), F32),
        scratch_shapes=[pltpu.VMEM((HY_ORDER, W // LANES, M * SUBLANES, LANES), F32)],
        compiler_params=_params("arbitrary"),
        name="filter_dft",
    )(w1[:, 0:1], w1[:, 1:1 + nb], w1[:, 1 + nb:], hy_b1[layer].reshape(HY_FFN, 1), hy_freq[layer].T,
      hy_w2[layer].T, hy_b2[layer].reshape(HY_FFN, 1), hy_w3, hy_w3, hy_w3, hy_w3,
      hy_skip.reshape(hy_skip.shape[0], HY_ORDER, 1, W), absdelta, fa3)


def _outer_dft_kernel(f_ref, x_ref, *rest, gated):
    gate_ref = rest[0] if gated else None
    o_ref, xs_ref, ys_ref = rest[-3:]
    K, sub, tw = x_ref.shape
    M = o_ref.shape[0]
    nch = tw // LANES
    for c in range(nch):
        xs_ref[c] = x_ref[:, :, c * LANES:(c + 1) * LANES].reshape(K * sub, LANES)
    f3 = f_ref[...]
    for s in range(sub):
        x = jnp.concatenate([xs_ref[c, pl.ds(s, K, stride=sub), :] for c in range(nch)], axis=1)
        y = _dot3(f3, x)
        for c in range(nch):
            ys_ref[c, pl.ds(s, M, stride=sub), :] = y[:, c * LANES:(c + 1) * LANES]
    for c in range(nch):
        y = ys_ref[c].reshape(M, sub, LANES)
        if gated:
            y = y * gate_ref[:, :, c * LANES:(c + 1) * LANES]
        o_ref[:, :, c * LANES:(c + 1) * LANES] = y


def _outer_dft(f3, x, xb, gate=None, gb=0, tw=BRANCH_W):
    _, K, N2, W = x.shape
    M = f3.shape[0]
    B = x.shape[0] if xb is None else 1
    x_idx = (lambda b: b) if xb is None else (lambda b: xb)
    blk = lambda rows, idx: pl.BlockSpec((None, rows, SUBLANES, tw), lambda b, t, c: (idx(b), 0, t, c))
    in_specs = [pl.BlockSpec((M, 3 * K), lambda b, t, c: (0, 0)), blk(K, x_idx)]
    args = [f3, x]
    if gate is not None:
        in_specs.append(blk(M, lambda b: gb))
        args.append(gate)
    return pl.pallas_call(
        functools.partial(_outer_dft_kernel, gated=gate is not None),
        grid=(B, N2 // SUBLANES, W // tw),
        in_specs=in_specs,
        out_specs=blk(M, lambda b: b),
        out_shape=jax.ShapeDtypeStruct((B, M, N2, W), F32),
        scratch_shapes=[pltpu.VMEM((tw // LANES, K * SUBLANES, LANES), F32),
                        pltpu.VMEM((tw // LANES, M * SUBLANES, LANES), F32)],
        compiler_params=_params("parallel", "parallel", "parallel"),
        name="dft_outer",
    )(*args)


def _conv_mid_kernel(gf_ref, gi_ref, a_ref, f_ref, o_ref):
    half = a_ref.shape[0] // 2
    gf3 = gf_ref[...]
    x = _dot3(gf3, a_ref[...])
    h = _dot3(gf3, f_ref[...])
    xr, xi = x[:half], x[half:]
    hr, hi = h[:half], h[half:]
    y = jnp.concatenate([xr * hr - xi * hi, xr * hi + xi * hr], axis=0)
    o_ref[...] = _dot3(gi_ref[...], y)


def _conv_mid(gf3, gi3, a, hspec, order):
    N1, R, W = a.shape
    return pl.pallas_call(
        _conv_mid_kernel,
        grid=(N1,),
        in_specs=[pl.BlockSpec((None, R, 3 * R), lambda k: (k, 0, 0)),
                  pl.BlockSpec((None, R, 3 * R), lambda k: (k, 0, 0)),
                  pl.BlockSpec((None, R, W), lambda k: (k, 0, 0)),
                  pl.BlockSpec((None, None, R, W), lambda k: (order, k, 0, 0))],
        out_specs=pl.BlockSpec((None, R, W), lambda k: (k, 0, 0)),
        out_shape=jax.ShapeDtypeStruct(a.shape, F32),
        compiler_params=_params("parallel"),
        name="conv_mid",
    )(gf3, gi3, a, hspec)


def _dft_tables(L):
    N = 2 * L
    N2 = LANES
    N1 = N // N2
    NS = pl.cdiv(N1 // 2 + 1, SUBLANES) * SUBLANES
    two_pi = 2.0 * math.pi
    ks = jnp.arange(NS, dtype=jnp.int32)
    t1 = jnp.arange(N1, dtype=jnp.int32)
    live = (ks <= N1 // 2).astype(F32)
    ang = ((ks[:, None] * t1[None, :]) % N1).astype(F32) * (two_pi / N1)
    cs = jnp.stack([jnp.cos(ang), -jnp.sin(ang)], axis=1) * live[:, None, None]
    fa = cs.reshape(2 * NS, N1)
    weight = jnp.where((ks == 0) | (ks == N1 // 2), 1.0, 2.0) / N
    fi = jnp.transpose(cs * weight[:, None, None], (2, 0, 1)).reshape(N1, 2 * NS)[:N1 // 2]
    t2 = jnp.arange(N2, dtype=jnp.int32)
    k = ks[:, None, None] + N1 * t2[None, :, None]
    ph = ((k * t2[None, None, :]) % N).astype(F32) * (two_pi / N)
    gr, gim = jnp.cos(ph), -jnp.sin(ph)
    gf = jnp.concatenate([jnp.concatenate([gr, -gim], axis=2), jnp.concatenate([gim, gr], axis=2)], axis=1)
    gi = jnp.swapaxes(gf, 1, 2)
    return {"fa": _const3(fa), "fh": _const3(fa[:, :N1 // 2]), "fi": _const3(fi), "gf": _const3(gf),
            "gi": _const3(gi), "N1": N1, "N2": N2, "NS": NS}


def _long_conv(z, zi, gate, gi_, hspec, order, tb):
    _, S, W = z.shape
    N1, N2, NS = tb["N1"], tb["N2"], tb["NS"]
    split = lambda t: t.reshape(t.shape[0], N1 // 2, N2, W)
    a = _outer_dft(tb["fh"], split(z), zi)
    b = _conv_mid(tb["gf"], tb["gi"], a.reshape(NS, 2 * N2, W), hspec, order)
    y = _outer_dft(tb["fi"], b.reshape(1, 2 * NS, N2, W), 0, split(gate), gi_)
    return y.reshape(1, S, W)


def _merge_kernel(a_ref, zb_ref, bg_ref, c_ref, w_ref, ma_ref, mb_ref, mc_ref, bias_ref, o_ref):
    b = (zb_ref[...] * _silu(bg_ref[...])).astype(BF16)
    bias = bias_ref[...]
    y = None
    for n, (br, m_ref) in enumerate(((a_ref[...], ma_ref), (b, mb_ref), (c_ref[...], mc_ref))):
        pb = jnp.dot(br, w_ref[n], preferred_element_type=F32)
        gate = _sigmoid(m_ref[...] + bias[n:n + 1])
        y = gate * pb if y is None else y + gate * pb
    o_ref[...] = y.astype(o_ref.dtype)


def _merge(a_out, zb, p1, c_out, mg, w_proj_bf, merge_b, layer, tm=256):
    S = a_out.shape[0]
    W, D = BRANCH_W, D_MODEL
    bgb = (P1_COLS - W) // W
    branch = pl.BlockSpec((tm, W), lambda i: (i, 0))
    mspec = lambda n: pl.BlockSpec((tm, D), lambda i: (i, n))
    return pl.pallas_call(
        _merge_kernel,
        grid=(S // tm,),
        in_specs=[branch, pl.BlockSpec((None, tm, W), lambda i: (0, i, 0)),
                  pl.BlockSpec((tm, W), lambda i: (i, bgb)), branch,
                  pl.BlockSpec((None, N_BRANCH, W, D), lambda i: (layer, 0, 0, 0)),
                  mspec(0), mspec(1), mspec(2),
                  pl.BlockSpec((None, N_BRANCH, D), lambda i: (layer, 0, 0))],
        out_specs=pl.BlockSpec((tm, D), lambda i: (i, 0)),
        out_shape=jax.ShapeDtypeStruct((S, D), BF16),
        compiler_params=_params("parallel"),
        name="merge",
    )(a_out, zb, p1, c_out, w_proj_bf, mg, mg, mg, merge_b)


def _out_kernel(x_ref, y_ref, w_ref, g_ref, xo_ref, ho_ref):
    x = x_ref[...] + jnp.dot(y_ref[...], w_ref[...], preferred_element_type=F32)
    xo_ref[...] = x
    ms = jnp.mean(x * x, axis=-1, keepdims=True)
    ho_ref[...] = (x * lax.rsqrt(ms + NORM_EPS) * g_ref[...]).astype(ho_ref.dtype)


def _out_proj(x, y, w_out_bf, layer, g, h_dtype, tm=512):
    S, D = x.shape
    row = pl.BlockSpec((tm, D), lambda i: (i, 0))
    return pl.pallas_call(
        _out_kernel,
        grid=(S // tm,),
        in_specs=[row, row, pl.BlockSpec((None, D, D), lambda i: (layer, 0, 0)), pl.BlockSpec((1, D), lambda i: (0, 0))],
        out_specs=(row, row),
        out_shape=(jax.ShapeDtypeStruct((S, D), F32), jax.ShapeDtypeStruct((S, D), h_dtype)),
        compiler_params=_params("parallel"),
        name="out_proj",
    )(x, y, w_out_bf, g.reshape(1, D))


def _encode(xb, norm_g, final_g, w_in, merge_b, rel_bias, hy_conv, hy_w1, hy_b1, hy_freq, hy_w2, hy_b2, hy_w3,
            hy_skip, diff_lam, diff_g, w_proj_bf, w_out_bf, tb):
    S = xb.shape[0]
    depth = w_in.shape[0]
    N1, N2 = tb["N1"], tb["N2"]
    h = _rmsnorm(xb, norm_g[0], BF16)
    for l in range(depth):
        qkv = [_in_proj(h, w_in, l, COL_A + g * A_GROUP_COLS, A_GROUP_COLS, dil=DIL_PAIRS[g][1]) for g in range(3)]
        p1 = _in_proj(h, w_in, l, COL_P1, P1_COLS)
        cqkv = _in_proj(h, w_in, l, COL_C, C_QKV_COLS, dil=1, scale0=C_Q_SCALE)
        cg = _in_proj(h, w_in, l, COL_P2, BRANCH_W)
        mg = _in_proj(h, w_in, l, COL_P2 + BRANCH_W, N_BRANCH * D_MODEL)

        others = [_dilated_group(rel_bias, qkv[g], g, None, None) for g in (1, 2)]
        a_out = _dilated_group(rel_bias, qkv[0], 0, others, p1, tq=A_CHAINS * A_SUB)

        u = _short_conv(p1, hy_conv, l, BRANCH_W)
        fa_out = _filter_outer_dft(S, tb["fa"], hy_w1, hy_b1, hy_freq, hy_w2, hy_b2, hy_w3, hy_skip, l)
        hspec = fa_out.reshape(HY_ORDER, tb["NS"], 2 * N2, BRANCH_W)
        z = _long_conv(u, 0, u, 1, hspec, 0, tb)
        zb = _long_conv(z, 0, u, 2, hspec, 1, tb)

        c_out = _diff_attention(tb["strips"], cqkv, diff_lam, diff_g, l, cg, 0)

        y = _merge(a_out, zb, p1, c_out, mg, w_proj_bf, merge_b, l)
        if l + 1 < depth:
            xb, h = _out_proj(xb, y, w_out_bf, l, norm_g[l + 1], BF16)
        else:
            _, h = _out_proj(xb, y, w_out_bf, l, final_g, F32)
    return h


def kernel(x, norm_g, final_g, w_in, merge_b, rel_bias, hy_conv, hy_w1, hy_b1, hy_freq, hy_w2, hy_b2, hy_w3,
           hy_skip, diff_lam, diff_g, w_proj, w_out):
    B, S, D = x.shape
    w_proj_bf = w_proj.astype(BF16)
    w_out_bf = w_out.astype(BF16)
    tb = _dft_tables(S)
    tq, _, dmin, _, width = _strip_geometry(S)
    tb["strips"] = _bias_strips(rel_bias, tq, dmin, width)
    outs = [_encode(x[b], norm_g, final_g, w_in, merge_b, rel_bias, hy_conv, hy_w1, hy_b1, hy_freq, hy_w2, hy_b2,
                    hy_w3, hy_skip, diff_lam, diff_g, w_proj_bf, w_out_bf, tb) for b in range(B)]
    return outs[0].reshape(1, S, D) if B == 1 else jnp.stack(outs)
```

```python
import functools
import math

import jax
import jax.numpy as jnp
from jax import lax
from jax.experimental import pallas as pl
from jax.experimental.pallas import tpu as pltpu

F32 = jnp.float32
BF16 = jnp.bfloat16

LANES = 128
SUBLANES = 8
D_MODEL = 2048
BRANCH_W = 1024
N_BRANCH = 3
DIL_PAIRS = ((128, 1), (512, 4), (2048, 16))
A_HEADS = 8
A_HEAD_DIM = 128
REACH = 64
A_SUB = 128
A_KW = A_SUB + 2 * REACH
A_CHAINS = 8
C_HEADS = 8
C_QK_DIM = 64
C_V_DIM = 128
HY_BANDS = 16
HY_FFN = 64
HY_ORDER = 2
HY_TARGET = 1e-2
HY_FAST_PCT = 0.3
HY_SLOW_PCT = 1.5
N_BUCKETS = 32
REL_MAX_DIST = 1024
NORM_EPS = 1e-6
NEG = -1e30

A_GROUP_COLS = 3 * A_HEADS * A_HEAD_DIM
A_QKV_COLS = 3 * A_GROUP_COLS
P1_COLS = BRANCH_W + 3 * BRANCH_W + BRANCH_W
C_QKV_COLS = 3 * C_HEADS * C_V_DIM
P2_COLS = BRANCH_W + N_BRANCH * D_MODEL
COL_A, COL_P1, COL_C, COL_P2 = 0, A_QKV_COLS, A_QKV_COLS + P1_COLS, A_QKV_COLS + P1_COLS + C_QKV_COLS

VMEM_LIMIT = 56 * 1024 * 1024
HIGHEST = lax.Precision.HIGHEST
LOG2E = math.log2(math.e)
C_Q_SCALE = LOG2E / math.sqrt(C_QK_DIM)

def _bucket_thresholds():
    nb = N_BUCKETS // 2
    max_exact = nb // 2
    thr = []
    for k in range(1, nb - max_exact):
        n = max_exact
        while int(math.log(n / max_exact) / math.log(REL_MAX_DIST / max_exact) * (nb - max_exact)) < k:
            n += 1
        thr.append(n)
    return tuple(thr)


BUCKET_THR = _bucket_thresholds()
FAR_DIST = BUCKET_THR[-1]


def _params(*sem):
    return pltpu.CompilerParams(dimension_semantics=sem, vmem_limit_bytes=VMEM_LIMIT)


def _rel_bias(rel, tab_ref, col):
    nb = N_BUCKETS // 2
    n = jnp.abs(rel)
    large = jnp.full(rel.shape, nb // 2, jnp.int32)
    for thr in BUCKET_THR:
        large = large + (n >= thr).astype(jnp.int32)
    bucket = jnp.where(n < nb // 2, n, large) + jnp.where(rel > 0, nb, 0)
    out = jnp.full(rel.shape, tab_ref[0, col], F32)
    for j in range(1, N_BUCKETS):
        out = jnp.where(bucket == j, tab_ref[j, col], out)
    return out


def _sigmoid(x):
    return 0.5 * jnp.tanh(0.5 * x) + 0.5


def _silu(x):
    return x * _sigmoid(x)


def _split_bf16(x):
    hi = x.astype(BF16)
    return hi, (x - hi.astype(F32)).astype(BF16)


def _const3(a):
    hi, lo = _split_bf16(a)
    return jnp.concatenate([hi, hi, lo], axis=-1)


def _dot3(a3, x):
    hi, lo = _split_bf16(x)
    return jnp.dot(a3, jnp.concatenate([hi, lo, hi], axis=0), preferred_element_type=F32)


def _dot3_both(a, x):
    return _dot3(_const3(a), x)


def _rmsnorm_kernel(x_ref, g_ref, o_ref):
    x = x_ref[...]
    ms = jnp.mean(x * x, axis=-1, keepdims=True)
    o_ref[...] = (x * lax.rsqrt(ms + NORM_EPS) * g_ref[...]).astype(o_ref.dtype)


def _rmsnorm(x, g, out_dtype, tm=512):
    S, D = x.shape
    return pl.pallas_call(
        _rmsnorm_kernel,
        grid=(S // tm,),
        in_specs=[pl.BlockSpec((tm, D), lambda i: (i, 0)), pl.BlockSpec((1, D), lambda i: (0, 0))],
        out_specs=pl.BlockSpec((tm, D), lambda i: (i, 0)),
        out_shape=jax.ShapeDtypeStruct((S, D), out_dtype),
        compiler_params=_params("parallel"),
        name="rmsnorm",
    )(x, g.reshape(1, D))


def _proj_kernel(h_ref, w_ref, o_ref, wb_ref, *acc_ref, dil, scale0, kchunks):
    def emit(acc):
        if scale0 is not None:
            acc = acc * jnp.where(pl.program_id(0) == 0, scale0, 1.0)
        tm, tn = acc.shape
        if dil is None:
            o_ref[...] = acc
        elif dil == 1:
            for c in range(tn // LANES):
                o_ref[c, 0] = acc[:, c * LANES:(c + 1) * LANES].astype(o_ref.dtype)
        else:
            for c in range(tn // LANES):
                acc_ref[0][c] = acc[:, c * LANES:(c + 1) * LANES]
                for r in range(dil):
                    o_ref[c, r] = acc_ref[0][c, pl.ds(r, tm // dil, stride=dil), :].astype(o_ref.dtype)

    @pl.when(pl.program_id(1) == 0)
    def _():
        kc = w_ref.shape[0] // kchunks
        acc = None
        for c in range(kchunks):
            rows = slice(c * kc, (c + 1) * kc)
            wc = w_ref[rows, :].astype(BF16)
            wb_ref[rows, :] = wc
            part = jnp.dot(h_ref[:, rows], wc, preferred_element_type=F32)
            acc = part if acc is None else acc + part
        emit(acc)

    @pl.when(pl.program_id(1) > 0)
    def _():
        emit(jnp.dot(h_ref[...], wb_ref[...], preferred_element_type=F32))


def _in_proj(h, w_in, layer, col0, ncols, dil=None, scale0=None, tm=1024, tn=1024):
    S, D = h.shape
    assert col0 % tn == 0 and ncols % tn == 0
    jb = col0 // tn
    scratch = [pltpu.VMEM((D, tn), BF16)]
    if dil is None:
        out_shape = jax.ShapeDtypeStruct((S, ncols), F32)
        out_spec = pl.BlockSpec((tm, tn), lambda j, i: (i, j))
    else:
        out_shape = jax.ShapeDtypeStruct((ncols // LANES, dil, S // dil, LANES), BF16)
        out_spec = pl.BlockSpec((tn // LANES, dil, tm // dil, LANES), lambda j, i: (j, 0, i, 0))
        if dil > 1:
            scratch.append(pltpu.VMEM((tn // LANES, tm, LANES), F32))
    return pl.pallas_call(
        functools.partial(_proj_kernel, dil=dil, scale0=scale0, kchunks=4),
        grid=(ncols // tn, S // tm),
        in_specs=[pl.BlockSpec((tm, D), lambda j, i: (i, 0)),
                  pl.BlockSpec((None, D, tn), lambda j, i: (layer, 0, jb + j))],
        out_specs=out_spec,
        out_shape=out_shape,
        scratch_shapes=scratch,
        compiler_params=_params("parallel", "arbitrary"),
        name="in_proj",
    )(h, w_in)


def _dil_kernel(*refs, dil, col0, others, n, tq):
    it = iter(refs)
    tab_ref, q_ref, k_ref, v_ref = next(it), next(it), next(it), next(it)
    other_refs = [(next(it), next(it)) for _ in others]
    gate_ref = next(it) if others else None
    o_ref = next(it)
    lse_ref = None if others else next(it)
    bm_ref = next(it)
    pos_refs = [(next(it), next(it)) for _ in others]

    h, r, i = pl.program_id(0), pl.program_id(1), pl.program_id(2)
    deltas = (-REACH, 0, -2 * REACH)

    @pl.when((r == 0) & (i == 0))
    def _():
        t = lax.broadcasted_iota(jnp.int32, (A_SUB, A_KW), 0)
        u = lax.broadcasted_iota(jnp.int32, (A_SUB, A_KW), 1)
        for idx, delta in enumerate(deltas):
            rel = u + delta - t
            bias = _rel_bias(rel * dil, tab_ref, col0 + h)
            bm_ref[idx] = jnp.where(jnp.abs(rel) <= REACH, bias, NEG)

    for (src_o, src_l), (dst_o, dst_l), d in zip(other_refs, pos_refs, others):
        for rr in range(d):
            dst_o[pl.ds(rr, tq // d, stride=d), :] = src_o[rr]
            dst_l[pl.ds(rr, tq // d, stride=d), :] = src_l[rr]

    rb = q_ref.shape[0]
    subs = [(rr, sb) for rr in range(rb) for sb in range(tq // A_SUB)]
    row_sl = [slice(sb * A_SUB, (sb + 1) * A_SUB) for _, sb in subs]
    m0s = [i * tq + sb * A_SUB for _, sb in subs]
    kss = [pl.multiple_of(jnp.clip(m0 - REACH, 0, n - A_KW), REACH) for m0 in m0s]
    sels = [jnp.where(m0 == 0, 1, jnp.where(m0 == n - A_SUB, 2, 0)) for m0 in m0s]
    ss = [lax.dot_general(q_ref[rr, rows, :], k_ref[rr, pl.ds(ks, A_KW), :], (((1,), (1,)), ((), ())),
                          preferred_element_type=F32) for (rr, _), rows, ks in zip(subs, row_sl, kss)]
    ss = [s * (1.0 / math.sqrt(A_HEAD_DIM)) + bm_ref[sel] for s, sel in zip(ss, sels)]
    ms = [jnp.max(s, axis=-1, keepdims=True) for s in ss]
    ps = [jnp.exp(s - m) for s, m in zip(ss, ms)]
    ls = [jnp.sum(p, axis=-1, keepdims=True) for p in ps]
    pvs = [jnp.dot(p.astype(BF16), v_ref[rr, pl.ds(ks, A_KW), :], preferred_element_type=F32)
           for (rr, _), p, ks in zip(subs, ps, kss)]
    for (rr, _), rows, m, l, pv in zip(subs, row_sl, ms, ls, pvs):
        o = pv / l
        lse = jnp.broadcast_to(m + jnp.log(l), o.shape)
        if others:
            lses = [lse] + [dst_l[rows, :] for _, dst_l in pos_refs]
            outs = [o] + [dst_o[rows, :] for dst_o, _ in pos_refs]
            mx = functools.reduce(jnp.maximum, lses)
            ws = [jnp.exp(x - mx) for x in lses]
            o = sum(w * x for w, x in zip(ws, outs)) / sum(ws)
            o_ref[rows, :] = (o * _silu(gate_ref[rows, :])).astype(o_ref.dtype)
        else:
            o_ref[rr, rows, :] = o
            lse_ref[rr, rows, :] = lse


def _dilated_group(tab, qkv, g, others, gate_src, tq=512):
    win, dil = DIL_PAIRS[g]
    assert win // (2 * dil) == REACH
    _, _, n, hd = qkv.shape
    tq = min(tq, n)
    rb = min(dil, A_CHAINS * A_SUB // tq)
    assert qkv.shape[1] == dil and n % tq == 0 and n >= A_KW and dil % rb == 0
    last = others is not None
    qblk = pl.BlockSpec((None, rb, tq, hd), lambda h, r, i: (h, r, i, 0))
    in_specs = [pl.BlockSpec(memory_space=pltpu.SMEM), qblk,
                pl.BlockSpec((None, rb, n, hd), lambda h, r, i: (A_HEADS + h, r, 0, 0)),
                pl.BlockSpec((None, rb, n, hd), lambda h, r, i: (2 * A_HEADS + h, r, 0, 0))]
    args = [tab, qkv, qkv, qkv]
    scratch = [pltpu.VMEM((3, A_SUB, A_KW), F32)]
    other_dils = ()
    if last:
        assert dil == 1
        other_dils = tuple(o.shape[1] for o, _ in others)
        for (o, lse), d in zip(others, other_dils):
            spec = pl.BlockSpec((None, d, tq // d, hd), lambda h, r, i: (h, 0, i, 0))
            in_specs += [spec, spec]
            args += [o, lse]
            scratch += [pltpu.VMEM((tq, hd), F32), pltpu.VMEM((tq, hd), F32)]
        in_specs.append(pl.BlockSpec((tq, hd), lambda h, r, i: (i, h)))
        args.append(gate_src)
        out_shape = jax.ShapeDtypeStruct((n, A_HEADS * hd), BF16)
        out_specs = pl.BlockSpec((tq, hd), lambda h, r, i: (i, h))
    else:
        out_shape = (jax.ShapeDtypeStruct((A_HEADS, dil, n, hd), F32),) * 2
        out_specs = (qblk, qblk)
    return pl.pallas_call(
        functools.partial(_dil_kernel, dil=dil, col0=g * A_HEADS, others=other_dils, n=n, tq=tq),
        grid=(A_HEADS, dil // rb, n // tq),
        in_specs=in_specs,
        out_specs=out_specs,
        out_shape=out_shape,
        scratch_shapes=scratch,
        compiler_params=_params("arbitrary", "arbitrary", "arbitrary"),
        name=f"dilated_attn_g{g}",
    )(*args)


def _strip_kernel(tab_ref, o_ref, *, dmin, rows):
    h, r = pl.program_id(0), pl.program_id(1)
    width = o_ref.shape[1]
    col = 3 * A_HEADS + h
    far_lo = tab_ref[N_BUCKETS // 2 - 1, col] * LOG2E
    far_hi = tab_ref[N_BUCKETS - 1, col] * LOG2E
    r0 = r * rows
    band = pl.cdiv(rows + 2 * FAR_DIST, LANES) * LANES + LANES
    lo = pl.multiple_of(jnp.clip((r0 - dmin - FAR_DIST) // LANES * LANES, 0, width - band), LANES)
    c_all = lax.broadcasted_iota(jnp.int32, (rows, width), 1)
    t_all = r0 + lax.broadcasted_iota(jnp.int32, (rows, width), 0)
    o_ref[...] = jnp.where(c_all - t_all + dmin < 0, far_lo, far_hi)
    t = r0 + lax.broadcasted_iota(jnp.int32, (rows, band), 0)
    c = lo + lax.broadcasted_iota(jnp.int32, (rows, band), 1)
    o_ref[:, pl.ds(lo, band)] = _rel_bias(c - t + dmin, tab_ref, col) * LOG2E


def _bias_strips(tab, tq, dmin, width, rows=64):
    assert width >= pl.cdiv(rows + 2 * FAR_DIST, LANES) * LANES + LANES
    return pl.pallas_call(
        functools.partial(_strip_kernel, dmin=dmin, rows=rows),
        grid=(C_HEADS, tq // rows),
        in_specs=[pl.BlockSpec(memory_space=pltpu.SMEM)],
        out_specs=pl.BlockSpec((None, rows, width), lambda h, r: (h, r, 0)),
        out_shape=jax.ShapeDtypeStruct((C_HEADS, tq, width), F32),
        compiler_params=_params("parallel", "parallel"),
        name="bias_strips",
    )(tab)


def _diff_kernel(strip_ref, dl_ref, g_ref, q_ref, k_ref, v_ref, gate_ref, o_ref, v2_ref, *,
                 lam_init, S, tq, tk, nsub, unroll, dmin, dmax):
    i = pl.program_id(1)
    nk = S // tk
    tks = tk // nsub

    @pl.when(i == 0)
    def _():
        v2_ref[:, :C_V_DIM] = v_ref[...]
        v2_ref[:, C_V_DIM:] = jnp.ones((S, C_V_DIM), BF16)

    q = q_ref[...]
    lane = lax.broadcasted_iota(jnp.int32, q.shape, 1)
    zero = jnp.zeros_like(q)
    qs = (jnp.where(lane < C_QK_DIM, q, zero), jnp.where(lane >= C_QK_DIM, q, zero))

    def logits(j, u):
        k0 = pl.multiple_of(j * tk, tk)
        b0 = pl.multiple_of(jnp.clip(j * tk - i * tq, dmin, dmax) - dmin, LANES)
        kt = k_ref[pl.ds(k0 + u * tks, tks), :]
        b = strip_ref[:, pl.ds(b0 + u * tks, tks)]
        return [lax.dot_general(qc, kt, (((1,), (1,)), ((), ())), preferred_element_type=F32) + b for qc in qs]

    def values(j, u):
        return v2_ref[pl.ds(pl.multiple_of(j * tk, tk) + u * tks, tks), :]

    def key_tiles(first, carry):
        steps = [(first * unroll + jj, u) for jj in range(unroll) for u in range(nsub)]
        ahead = logits(*steps[0])
        for idx, (j, u) in enumerate(steps):
            ss, ahead = ahead, (logits(*steps[idx + 1]) if idx + 1 < len(steps) else None)
            ms = [jnp.maximum(m, jnp.max(s, axis=-1, keepdims=True)) for s, (m, _) in zip(ss, carry)]
            ps = [jnp.exp2((s - m_new).astype(BF16)) for s, m_new in zip(ss, ms)]
            pvs = [jnp.dot(p, values(j, u), preferred_element_type=F32) for p in ps]
            carry = tuple((m_new, jnp.exp2(m - m_new) * acc + pv) for m_new, pv, (m, acc) in zip(ms, pvs, carry))
        return carry

    start = (jnp.full((tq, 1), NEG, F32), jnp.zeros((tq, 2 * C_V_DIM), F32))
    if nk == unroll:
        carry = key_tiles(0, (start, start))
    else:
        carry = lax.fori_loop(0, nk // unroll, key_tiles, (start, start))
    on = [acc[:, :C_V_DIM] / acc[:, C_V_DIM:] for _, acc in carry]
    dl = dl_ref[...]
    lam = (jnp.exp(jnp.sum(dl[0:1] * dl[1:2], axis=-1, keepdims=True))
           - jnp.exp(jnp.sum(dl[2:3] * dl[3:4], axis=-1, keepdims=True)) + lam_init)
    o = on[0] - lam * on[1]
    y = o * lax.rsqrt(jnp.mean(o * o, axis=-1, keepdims=True) + NORM_EPS) * g_ref[...]
    o_ref[...] = (y * (1.0 - lam_init) * _silu(gate_ref[...])).astype(o_ref.dtype)


C_TQ, C_TK, C_NSUB = 512, 2048, 4


def _strip_geometry(S):
    tq, tk = min(C_TQ, S), min(C_TK, S)
    dmax = pl.cdiv(FAR_DIST + tq - 1, LANES) * LANES
    dmin = -pl.cdiv(FAR_DIST + tk - 1, LANES) * LANES
    return tq, tk, dmin, dmax, dmax - dmin + tk


def _diff_attention(strips, cqkv, diff_lam, diff_g, layer, gate_src, gate_col):
    _, _, S, hd = cqkv.shape
    tq, tk, dmin, dmax, width = _strip_geometry(S)
    nsub = C_NSUB
    assert S % tk == 0 and S % tq == 0 and strips.shape == (C_HEADS, tq, width)
    lam_init = 0.8 - 0.6 * math.exp(-0.3 * layer)
    gcb = gate_col // hd
    return pl.pallas_call(
        functools.partial(_diff_kernel, lam_init=lam_init, S=S, tq=tq, tk=tk, nsub=nsub,
                          unroll=math.gcd(S // tk, 4), dmin=dmin, dmax=dmax),
        grid=(C_HEADS, S // tq),
        in_specs=[pl.BlockSpec((None, tq, width), lambda h, i: (h, 0, 0)),
                  pl.BlockSpec((None, 4, C_QK_DIM), lambda h, i: (layer, 0, 0)),
                  pl.BlockSpec((None, 1, C_V_DIM), lambda h, i: (layer, 0, 0)),
                  pl.BlockSpec((None, None, tq, hd), lambda h, i: (h, 0, i, 0)),
                  pl.BlockSpec((None, None, S, hd), lambda h, i: (C_HEADS + h, 0, 0, 0)),
                  pl.BlockSpec((None, None, S, hd), lambda h, i: (2 * C_HEADS + h, 0, 0, 0)),
                  pl.BlockSpec((tq, hd), lambda h, i: (i, gcb + h))],
        out_specs=pl.BlockSpec((tq, hd), lambda h, i: (i, h)),
        out_shape=jax.ShapeDtypeStruct((S, C_HEADS * hd), BF16),
        scratch_shapes=[pltpu.VMEM((S, 2 * C_V_DIM), BF16)],
        compiler_params=_params("arbitrary", "arbitrary"),
        name="diff_attn",
    )(strips, diff_lam, diff_g.reshape(diff_g.shape[0], 1, C_V_DIM), cqkv, cqkv, cqkv, gate_src)


def _short_conv_kernel(b_ref, w_ref, o_ref):
    b = b_ref[...]
    S = b.shape[0]
    row = lax.broadcasted_iota(jnp.int32, b.shape, 0)
    prev = jnp.where(row == 0, 0.0, pltpu.roll(b, 1, 0))
    nxt = jnp.where(row == S - 1, 0.0, pltpu.roll(b, S - 1, 0))
    w = w_ref[...]
    o_ref[...] = w[0:1] * prev + w[1:2] * b + w[2:3] * nxt


def _short_conv(p1, hy_conv, layer, col0):
    S = p1.shape[0]
    nct = BRANCH_W // LANES
    cb = col0 // LANES
    return pl.pallas_call(
        _short_conv_kernel,
        grid=(3 * nct,),
        in_specs=[pl.BlockSpec((S, LANES), lambda c: (0, cb + c)),
                  pl.BlockSpec((None, 3, LANES), lambda c: (layer, 0, c))],
        out_specs=pl.BlockSpec((None, S, LANES), lambda c: (c // nct, 0, c % nct)),
        out_shape=jax.ShapeDtypeStruct((3, S, BRANCH_W), F32),
        compiler_params=_params("parallel"),
        name="short_conv",
    )(p1, hy_conv)


def _filter_dft_kernel(w1t_ref, w1c_ref, w1s_ref, b1_ref, fr_ref, w2_ref, b2_ref, w30f_ref, w30b_ref, w31f_ref,
                       w31b_ref, skip_ref, ad_ref, fa_ref, o_ref, ys_ref, *, L):
    j = pl.program_id(0)
    N = 2 * L
    N1 = fa_ref.shape[1] // 3
    N2 = N // N1
    M = fa_ref.shape[0]
    nch = o_ref.shape[-1] // LANES

    def hidden(pos):
        t = pos * (1.0 / (L - 1))
        w = pos * (2.0 * math.pi / L)
        fb = (1e-4 + lax.broadcasted_iota(jnp.int32, (HY_BANDS, 1), 0).astype(F32)
              * ((HY_BANDS - 1 - 1e-4) / (HY_BANDS - 1)))
        ang = fb * w
        pre = (w1t_ref[...] * t
               + jnp.dot(w1c_ref[...], jnp.cos(ang), precision=HIGHEST, preferred_element_type=F32)
               - jnp.dot(w1s_ref[...], jnp.sin(ang), precision=HIGHEST, preferred_element_type=F32)
               + b1_ref[...])
        fr = fr_ref[...]
        hid = jnp.sin(fr[:, 0:1] * pre)
        hid = jnp.sin(fr[:, 1:2] * (jnp.dot(w2_ref[...], hid, precision=HIGHEST, preferred_element_type=F32)
                                    + b2_ref[...]))
        return hid.T

    def split_rows(x):
        hi, lo = _split_bf16(x)
        return jnp.concatenate([hi, lo, hi], axis=0)

    w3 = [[split_rows(ref[...]) for ref in pair] for pair in ((w30f_ref, w30b_ref), (w31f_ref, w31b_ref))]
    hid0 = _const3(jnp.broadcast_to(hidden(jnp.zeros((1, LANES), F32))[0:1], (SUBLANES, HY_FFN)))
    lag0 = [jnp.dot(hid0, w3[o][1], preferred_element_type=F32)[0:1] + skip_ref[o] for o in range(HY_ORDER)]
    fa3 = fa_ref[...]
    half = N1 // 2
    lag = lambda r: jnp.where(r < L, r, N - r).astype(F32)
    lane = lax.broadcasted_iota(jnp.int32, (1, SUBLANES * N1), 1)
    hid_all = hidden(lag((lane % N1) * N2 + j * SUBLANES + lane // N1))
    rows = [lax.broadcasted_iota(jnp.int32, (N1, 1), 0) * N2 + j * SUBLANES + s for s in range(SUBLANES)]
    hids = [(_const3(hid_all[s * N1:s * N1 + half]), _const3(hid_all[s * N1 + half:(s + 1) * N1]))
            for s in range(SUBLANES)]
    decays = [jnp.exp(-(lag(row) * (1.0 / (L - 1))) * ad_ref[...]) for row in rows]
    pairs = [(s, o) for s in range(SUBLANES) for o in range(HY_ORDER)]
    gs = [jnp.concatenate([jnp.dot(hids[s][0], w3[o][0], preferred_element_type=F32),
                           jnp.dot(hids[s][1], w3[o][1], preferred_element_type=F32)], axis=0) * decays[s]
          for s, o in pairs]
    gs = [jnp.where(rows[0] == L, 0.0, g + jnp.where(rows[0] == 0, lag0[o], 0.0)) if s == 0 else g
          for g, (s, o) in zip(gs, pairs)]
    ys = [_dot3(fa3, g) for g in gs]
    for y, (s, o) in zip(ys, pairs):
        for c in range(nch):
            ys_ref[o, c, pl.ds(s, M, stride=SUBLANES), :] = y[:, c * LANES:(c + 1) * LANES]
    for o in range(HY_ORDER):
        for c in range(nch):
            o_ref[o, :, :, c * LANES:(c + 1) * LANES] = ys_ref[o, c].reshape(M, SUBLANES, LANES)


def _filter_outer_dft(L, fa3, hy_w1, hy_b1, hy_freq, hy_w2, hy_b2, hy_w3, hy_skip, layer):
    assert HY_ORDER == 2
    N = 2 * L
    W = BRANCH_W
    nb = HY_BANDS
    M, N1 = fa3.shape[0], fa3.shape[1] // 3
    N2 = N // N1
    w1 = hy_w1[layer].T
    max_decay = math.log(HY_TARGET) / HY_FAST_PCT
    min_decay = math.log(HY_TARGET) / HY_SLOW_PCT
    absdelta = jnp.abs(jnp.linspace(min_decay, max_decay, W, dtype=F32)).reshape(1, W)
    full = lambda shape: pl.BlockSpec(shape, lambda j: (0,) * len(shape))
    w3_spec = lambda o, bwd: pl.BlockSpec((None, HY_FFN, W), lambda j: (layer, 0, 2 * o + bwd))
    return pl.pallas_call(
        functools.partial(_filter_dft_kernel, L=L),
        grid=(N2 // SUBLANES,),
        in_specs=[full((HY_FFN, 1)), full((HY_FFN, nb)), full((HY_FFN, nb)), full((HY_FFN, 1)), full((HY_FFN, 2)),
                  full((HY_FFN, HY_FFN)), full((HY_FFN, 1)),
                  w3_spec(0, 0), w3_spec(0, 1), w3_spec(1, 0), w3_spec(1, 1),
                  pl.BlockSpec((None, HY_ORDER, 1, W), lambda j: (layer, 0, 0, 0)),
                  full((1, W)), full((M, 3 * N1))],
        out_specs=pl.BlockSpec((HY_ORDER, M, SUBLANES, W), lambda j: (0, 0, j, 0)),
        out_shape=jax.ShapeDtypeStruct((HY_ORDER, M, N2, W), F32),
        scratch_shapes=[pltpu.VMEM((HY_ORDER, W // LANES, M * SUBLANES, LANES), F32)],
        compiler_params=_params("arbitrary"),
        name="filter_dft",
    )(w1[:, 0:1], w1[:, 1:1 + nb], w1[:, 1 + nb:], hy_b1[layer].reshape(HY_FFN, 1), hy_freq[layer].T,
      hy_w2[layer].T, hy_b2[layer].reshape(HY_FFN, 1), hy_w3, hy_w3, hy_w3, hy_w3,
      hy_skip.reshape(hy_skip.shape[0], HY_ORDER, 1, W), absdelta, fa3)


def _outer_dft_kernel(f_ref, x_ref, *rest, gated):
    gate_ref = rest[0] if gated else None
    o_ref, xs_ref, ys_ref = rest[-3:]
    K, sub, tw = x_ref.shape
    M = o_ref.shape[0]
    nch = tw // LANES
    for c in range(nch):
        xs_ref[c] = x_ref[:, :, c * LANES:(c + 1) * LANES].reshape(K * sub, LANES)
    f3 = f_ref[...]
    for s in range(sub):
        x = jnp.concatenate([xs_ref[c, pl.ds(s, K, stride=sub), :] for c in range(nch)], axis=1)
        y = _dot3(f3, x)
        for c in range(nch):
            ys_ref[c, pl.ds(s, M, stride=sub), :] = y[:, c * LANES:(c + 1) * LANES]
    for c in range(nch):
        y = ys_ref[c].reshape(M, sub, LANES)
        if gated:
            y = y * gate_ref[:, :, c * LANES:(c + 1) * LANES]
        o_ref[:, :, c * LANES:(c + 1) * LANES] = y


def _outer_dft(f3, x, xb, gate=None, gb=0, tw=BRANCH_W):
    _, K, N2, W = x.shape
    M = f3.shape[0]
    B = x.shape[0] if xb is None else 1
    x_idx = (lambda b: b) if xb is None else (lambda b: xb)
    blk = lambda rows, idx: pl.BlockSpec((None, rows, SUBLANES, tw), lambda b, t, c: (idx(b), 0, t, c))
    in_specs = [pl.BlockSpec((M, 3 * K), lambda b, t, c: (0, 0)), blk(K, x_idx)]
    args = [f3, x]
    if gate is not None:
        in_specs.append(blk(M, lambda b: gb))
        args.append(gate)
    return pl.pallas_call(
        functools.partial(_outer_dft_kernel, gated=gate is not None),
        grid=(B, N2 // SUBLANES, W // tw),
        in_specs=in_specs,
        out_specs=blk(M, lambda b: b),
        out_shape=jax.ShapeDtypeStruct((B, M, N2, W), F32),
        scratch_shapes=[pltpu.VMEM((tw // LANES, K * SUBLANES, LANES), F32),
                        pltpu.VMEM((tw // LANES, M * SUBLANES, LANES), F32)],
        compiler_params=_params("parallel", "parallel", "parallel"),
        name="dft_outer",
    )(*args)


def _conv_mid_kernel(gf_ref, gi_ref, a_ref, f_ref, o_ref, *, live):
    k = pl.program_id(0)

    @pl.when(k < live)
    def _():
        half = a_ref.shape[0] // 2
        gf3 = gf_ref[...]
        x = _dot3(gf3, a_ref[...])
        h = _dot3(gf3, f_ref[...])
        xr, xi = x[:half], x[half:]
        hr, hi = h[:half], h[half:]
        y = jnp.concatenate([xr * hr - xi * hi, xr * hi + xi * hr], axis=0)
        o_ref[...] = _dot3(gi_ref[...], y)

    @pl.when(k >= live)
    def _():
        o_ref[...] = jnp.zeros(o_ref.shape, F32)


def _conv_mid(gf3, gi3, a, hspec, order, live):
    NS, R, W = a.shape
    src = lambda k: jnp.minimum(k, live - 1)
    return pl.pallas_call(
        functools.partial(_conv_mid_kernel, live=live),
        grid=(NS,),
        in_specs=[pl.BlockSpec((None, R, 3 * R), lambda k: (src(k), 0, 0)),
                  pl.BlockSpec((None, R, 3 * R), lambda k: (src(k), 0, 0)),
                  pl.BlockSpec((None, R, W), lambda k: (src(k), 0, 0)),
                  pl.BlockSpec((None, None, R, W), lambda k: (order, src(k), 0, 0))],
        out_specs=pl.BlockSpec((None, R, W), lambda k: (k, 0, 0)),
        out_shape=jax.ShapeDtypeStruct(a.shape, F32),
        compiler_params=_params("parallel"),
        name="conv_mid",
    )(gf3, gi3, a, hspec)


def _dft_tables(L):
    N = 2 * L
    N2 = LANES
    N1 = N // N2
    NS = pl.cdiv(N1 // 2 + 1, SUBLANES) * SUBLANES
    two_pi = 2.0 * math.pi
    ks = jnp.arange(NS, dtype=jnp.int32)
    t1 = jnp.arange(N1, dtype=jnp.int32)
    live = (ks <= N1 // 2).astype(F32)
    ang = ((ks[:, None] * t1[None, :]) % N1).astype(F32) * (two_pi / N1)
    cs = jnp.stack([jnp.cos(ang), -jnp.sin(ang)], axis=1) * live[:, None, None]
    fa = cs.reshape(2 * NS, N1)
    weight = jnp.where((ks == 0) | (ks == N1 // 2), 1.0, 2.0) / N
    fi = jnp.transpose(cs * weight[:, None, None], (2, 0, 1)).reshape(N1, 2 * NS)[:N1 // 2]
    t2 = jnp.arange(N2, dtype=jnp.int32)
    k = ks[:, None, None] + N1 * t2[None, :, None]
    ph = ((k * t2[None, None, :]) % N).astype(F32) * (two_pi / N)
    gr, gim = jnp.cos(ph), -jnp.sin(ph)
    gf = jnp.concatenate([jnp.concatenate([gr, -gim], axis=2), jnp.concatenate([gim, gr], axis=2)], axis=1)
    gi = jnp.swapaxes(gf, 1, 2)
    return {"fa": _const3(fa), "fh": _const3(fa[:, :N1 // 2]), "fi": _const3(fi), "gf": _const3(gf),
            "gi": _const3(gi), "N1": N1, "N2": N2, "NS": NS}


def _long_conv(z, zi, gate, gi_, hspec, order, tb):
    _, S, W = z.shape
    N1, N2, NS = tb["N1"], tb["N2"], tb["NS"]
    split = lambda t: t.reshape(t.shape[0], N1 // 2, N2, W)
    a = _outer_dft(tb["fh"], split(z), zi)
    b = _conv_mid(tb["gf"], tb["gi"], a.reshape(NS, 2 * N2, W), hspec, order, N1 // 2 + 1)
    y = _outer_dft(tb["fi"], b.reshape(1, 2 * NS, N2, W), 0, split(gate), gi_)
    return y.reshape(1, S, W)


def _merge_kernel(a_ref, zb_ref, bg_ref, c_ref, w_ref, ma_ref, mb_ref, mc_ref, bias_ref, o_ref):
    b = (zb_ref[...] * _silu(bg_ref[...])).astype(BF16)
    bias = bias_ref[...]
    y = None
    for n, (br, m_ref) in enumerate(((a_ref[...], ma_ref), (b, mb_ref), (c_ref[...], mc_ref))):
        pb = jnp.dot(br, w_ref[n], preferred_element_type=F32)
        gate = _sigmoid(m_ref[...] + bias[n:n + 1])
        y = gate * pb if y is None else y + gate * pb
    o_ref[...] = y.astype(o_ref.dtype)


def _merge(a_out, zb, p1, c_out, mg, w_proj_bf, merge_b, layer, tm=256):
    S = a_out.shape[0]
    W, D = BRANCH_W, D_MODEL
    bgb = (P1_COLS - W) // W
    branch = pl.BlockSpec((tm, W), lambda i: (i, 0))
    mspec = lambda n: pl.BlockSpec((tm, D), lambda i: (i, n))
    return pl.pallas_call(
        _merge_kernel,
        grid=(S // tm,),
        in_specs=[branch, pl.BlockSpec((None, tm, W), lambda i: (0, i, 0)),
                  pl.BlockSpec((tm, W), lambda i: (i, bgb)), branch,
                  pl.BlockSpec((None, N_BRANCH, W, D), lambda i: (layer, 0, 0, 0)),
                  mspec(0), mspec(1), mspec(2),
                  pl.BlockSpec((None, N_BRANCH, D), lambda i: (layer, 0, 0))],
        out_specs=pl.BlockSpec((tm, D), lambda i: (i, 0)),
        out_shape=jax.ShapeDtypeStruct((S, D), BF16),
        compiler_params=_params("parallel"),
        name="merge",
    )(a_out, zb, p1, c_out, w_proj_bf, mg, mg, mg, merge_b)


def _out_kernel(x_ref, y_ref, w_ref, g_ref, xo_ref, ho_ref):
    x = x_ref[...] + jnp.dot(y_ref[...], w_ref[...], preferred_element_type=F32)
    xo_ref[...] = x
    ms = jnp.mean(x * x, axis=-1, keepdims=True)
    ho_ref[...] = (x * lax.rsqrt(ms + NORM_EPS) * g_ref[...]).astype(ho_ref.dtype)


def _out_proj(x, y, w_out_bf, layer, g, h_dtype, tm=512):
    S, D = x.shape
    row = pl.BlockSpec((tm, D), lambda i: (i, 0))
    return pl.pallas_call(
        _out_kernel,
        grid=(S // tm,),
        in_specs=[row, row, pl.BlockSpec((None, D, D), lambda i: (layer, 0, 0)), pl.BlockSpec((1, D), lambda i: (0, 0))],
        out_specs=(row, row),
        out_shape=(jax.ShapeDtypeStruct((S, D), F32), jax.ShapeDtypeStruct((S, D), h_dtype)),
        compiler_params=_params("parallel"),
        name="out_proj",
    )(x, y, w_out_bf, g.reshape(1, D))


def _encode(xb, norm_g, final_g, w_in, merge_b, rel_bias, hy_conv, hy_w1, hy_b1, hy_freq, hy_w2, hy_b2, hy_w3,
            hy_skip, diff_lam, diff_g, w_proj_bf, w_out_bf, tb):
    S = xb.shape[0]
    depth = w_in.shape[0]
    N1, N2 = tb["N1"], tb["N2"]
    h = _rmsnorm(xb, norm_g[0], BF16)
    for l in range(depth):
        qkv = [_in_proj(h, w_in, l, COL_A + g * A_GROUP_COLS, A_GROUP_COLS, dil=DIL_PAIRS[g][1]) for g in range(3)]
        p1 = _in_proj(h, w_in, l, COL_P1, P1_COLS)
        cqkv = _in_proj(h, w_in, l, COL_C, C_QKV_COLS, dil=1, scale0=C_Q_SCALE)
        cg = _in_proj(h, w_in, l, COL_P2, BRANCH_W)
        mg = _in_proj(h, w_in, l, COL_P2 + BRANCH_W, N_BRANCH * D_MODEL)

        others = [_dilated_group(rel_bias, qkv[g], g, None, None) for g in (1, 2)]
        a_out = _dilated_group(rel_bias, qkv[0], 0, others, p1, tq=A_CHAINS * A_SUB)

        u = _short_conv(p1, hy_conv, l, BRANCH_W)
        fa_out = _filter_outer_dft(S, tb["fa"], hy_w1, hy_b1, hy_freq, hy_w2, hy_b2, hy_w3, hy_skip, l)
        hspec = fa_out.reshape(HY_ORDER, tb["NS"], 2 * N2, BRANCH_W)
        z = _long_conv(u, 0, u, 1, hspec, 0, tb)
        zb = _long_conv(z, 0, u, 2, hspec, 1, tb)

        c_out = _diff_attention(tb["strips"], cqkv, diff_lam, diff_g, l, cg, 0)

        y = _merge(a_out, zb, p1, c_out, mg, w_proj_bf, merge_b, l)
        if l + 1 < depth:
            xb, h = _out_proj(xb, y, w_out_bf, l, norm_g[l + 1], BF16)
        else:
            _, h = _out_proj(xb, y, w_out_bf, l, final_g, F32)
    return h


def kernel(x, norm_g, final_g, w_in, merge_b, rel_bias, hy_conv, hy_w1, hy_b1, hy_freq, hy_w2, hy_b2, hy_w3,
           hy_skip, diff_lam, diff_g, w_proj, w_out):
    B, S, D = x.shape
    w_proj_bf = w_proj.astype(BF16)
    w_out_bf = w_out.astype(BF16)
    tb = _dft_tables(S)
    tq, _, dmin, _, width = _strip_geometry(S)
    tb["strips"] = _bias_strips(rel_bias, tq, dmin, width)
    outs = [_encode(x[b], norm_g, final_g, w_in, merge_b, rel_bias, hy_conv, hy_w1, hy_b1, hy_freq, hy_w2, hy_b2,
                    hy_w3, hy_skip, diff_lam, diff_g, w_proj_bf, w_out_bf, tb) for b in range(B)]
    return outs[0].reshape(1, S, D) if B == 1 else jnp.stack(outs)
```

```python
import functools
import math

import jax
import jax.numpy as jnp
from jax import lax
from jax.experimental import pallas as pl
from jax.experimental.pallas import tpu as pltpu

F32 = jnp.float32
BF16 = jnp.bfloat16

LANES = 128
SUBLANES = 8
D_MODEL = 2048
BRANCH_W = 1024
N_BRANCH = 3
DIL_PAIRS = ((128, 1), (512, 4), (2048, 16))
A_HEADS = 8
A_HEAD_DIM = 128
REACH = 64
A_SUB = 128
A_KW = A_SUB + 2 * REACH
A_CHAINS = 16
C_HEADS = 8
C_QK_DIM = 64
C_V_DIM = 128
HY_BANDS = 16
HY_FFN = 64
HY_ORDER = 2
HY_TARGET = 1e-2
HY_FAST_PCT = 0.3
HY_SLOW_PCT = 1.5
N_BUCKETS = 32
REL_MAX_DIST = 1024
NORM_EPS = 1e-6
NEG = -1e30

A_GROUP_COLS = 3 * A_HEADS * A_HEAD_DIM
A_QKV_COLS = 3 * A_GROUP_COLS
P1_COLS = BRANCH_W + 3 * BRANCH_W + BRANCH_W
C_QKV_COLS = 3 * C_HEADS * C_V_DIM
P2_COLS = BRANCH_W + N_BRANCH * D_MODEL
COL_A, COL_P1, COL_C, COL_P2 = 0, A_QKV_COLS, A_QKV_COLS + P1_COLS, A_QKV_COLS + P1_COLS + C_QKV_COLS

VMEM_LIMIT = 56 * 1024 * 1024
HIGHEST = lax.Precision.HIGHEST
LOG2E = math.log2(math.e)
C_Q_SCALE = LOG2E / math.sqrt(C_QK_DIM)

def _bucket_thresholds():
    nb = N_BUCKETS // 2
    max_exact = nb // 2
    thr = []
    for k in range(1, nb - max_exact):
        n = max_exact
        while int(math.log(n / max_exact) / math.log(REL_MAX_DIST / max_exact) * (nb - max_exact)) < k:
            n += 1
        thr.append(n)
    return tuple(thr)


BUCKET_THR = _bucket_thresholds()
FAR_DIST = BUCKET_THR[-1]


def _params(*sem):
    return pltpu.CompilerParams(dimension_semantics=sem, vmem_limit_bytes=VMEM_LIMIT)


def _rel_bias(rel, tab_ref, col):
    nb = N_BUCKETS // 2
    n = jnp.abs(rel)
    large = jnp.full(rel.shape, nb // 2, jnp.int32)
    for thr in BUCKET_THR:
        large = large + (n >= thr).astype(jnp.int32)
    bucket = jnp.where(n < nb // 2, n, large) + jnp.where(rel > 0, nb, 0)
    out = jnp.full(rel.shape, tab_ref[0, col], F32)
    for j in range(1, N_BUCKETS):
        out = jnp.where(bucket == j, tab_ref[j, col], out)
    return out


def _sigmoid(x):
    return 0.5 * jnp.tanh(0.5 * x) + 0.5


def _silu(x):
    return x * _sigmoid(x)


def _split_bf16(x):
    hi = x.astype(BF16)
    return hi, (x - hi.astype(F32)).astype(BF16)


def _const3(a):
    hi, lo = _split_bf16(a)
    return jnp.concatenate([hi, hi, lo], axis=-1)


def _dot3(a3, x):
    hi, lo = _split_bf16(x)
    return jnp.dot(a3, jnp.concatenate([hi, lo, hi], axis=0), preferred_element_type=F32)


def _dot3_both(a, x):
    return _dot3(_const3(a), x)


def _rmsnorm_kernel(x_ref, g_ref, o_ref):
    x = x_ref[...]
    ms = jnp.mean(x * x, axis=-1, keepdims=True)
    o_ref[...] = (x * lax.rsqrt(ms + NORM_EPS) * g_ref[...]).astype(o_ref.dtype)


def _rmsnorm(x, g, out_dtype, tm=512):
    S, D = x.shape
    return pl.pallas_call(
        _rmsnorm_kernel,
        grid=(S // tm,),
        in_specs=[pl.BlockSpec((tm, D), lambda i: (i, 0)), pl.BlockSpec((1, D), lambda i: (0, 0))],
        out_specs=pl.BlockSpec((tm, D), lambda i: (i, 0)),
        out_shape=jax.ShapeDtypeStruct((S, D), out_dtype),
        compiler_params=_params("parallel"),
        name="rmsnorm",
    )(x, g.reshape(1, D))


def _proj_kernel(h_ref, w_ref, o_ref, wb_ref, *acc_ref, dil, scale0, kchunks):
    def emit(acc):
        if scale0 is not None:
            acc = acc * jnp.where(pl.program_id(0) == 0, scale0, 1.0)
        tm, tn = acc.shape
        if dil is None:
            o_ref[...] = acc
        elif dil == 1:
            for c in range(tn // LANES):
                o_ref[c, 0] = acc[:, c * LANES:(c + 1) * LANES].astype(o_ref.dtype)
        else:
            for c in range(tn // LANES):
                acc_ref[0][c] = acc[:, c * LANES:(c + 1) * LANES]
                for r in range(dil):
                    o_ref[c, r] = acc_ref[0][c, pl.ds(r, tm // dil, stride=dil), :].astype(o_ref.dtype)

    @pl.when(pl.program_id(1) == 0)
    def _():
        kc = w_ref.shape[0] // kchunks
        acc = None
        for c in range(kchunks):
            rows = slice(c * kc, (c + 1) * kc)
            wc = w_ref[rows, :].astype(BF16)
            wb_ref[rows, :] = wc
            part = jnp.dot(h_ref[:, rows], wc, preferred_element_type=F32)
            acc = part if acc is None else acc + part
        emit(acc)

    @pl.when(pl.program_id(1) > 0)
    def _():
        emit(jnp.dot(h_ref[...], wb_ref[...], preferred_element_type=F32))


def _in_proj(h, w_in, layer, col0, ncols, dil=None, scale0=None, tm=1024, tn=1024):
    S, D = h.shape
    assert col0 % tn == 0 and ncols % tn == 0
    jb = col0 // tn
    scratch = [pltpu.VMEM((D, tn), BF16)]
    if dil is None:
        out_shape = jax.ShapeDtypeStruct((S, ncols), F32)
        out_spec = pl.BlockSpec((tm, tn), lambda j, i: (i, j))
    else:
        out_shape = jax.ShapeDtypeStruct((ncols // LANES, dil, S // dil, LANES), BF16)
        out_spec = pl.BlockSpec((tn // LANES, dil, tm // dil, LANES), lambda j, i: (j, 0, i, 0))
        if dil > 1:
            scratch.append(pltpu.VMEM((tn // LANES, tm, LANES), F32))
    return pl.pallas_call(
        functools.partial(_proj_kernel, dil=dil, scale0=scale0, kchunks=4),
        grid=(ncols // tn, S // tm),
        in_specs=[pl.BlockSpec((tm, D), lambda j, i: (i, 0)),
                  pl.BlockSpec((None, D, tn), lambda j, i: (layer, 0, jb + j))],
        out_specs=out_spec,
        out_shape=out_shape,
        scratch_shapes=scratch,
        compiler_params=_params("parallel", "arbitrary"),
        name="in_proj",
    )(h, w_in)


def _dil_kernel(*refs, dil, col0, others, n, tq):
    it = iter(refs)
    tab_ref, q_ref, k_ref, v_ref = next(it), next(it), next(it), next(it)
    other_refs = [(next(it), next(it)) for _ in others]
    gate_ref = next(it) if others else None
    o_ref = next(it)
    lse_ref = None if others else next(it)
    bm_ref = next(it)
    pos_refs = [(next(it), next(it)) for _ in others]

    h, r, i = pl.program_id(0), pl.program_id(1), pl.program_id(2)
    deltas = (-REACH, 0, -2 * REACH)

    @pl.when((r == 0) & (i == 0))
    def _():
        t = lax.broadcasted_iota(jnp.int32, (A_SUB, A_KW), 0)
        u = lax.broadcasted_iota(jnp.int32, (A_SUB, A_KW), 1)
        for idx, delta in enumerate(deltas):
            rel = u + delta - t
            bias = _rel_bias(rel * dil, tab_ref, col0 + h)
            bm_ref[idx] = jnp.where(jnp.abs(rel) <= REACH, bias, NEG)

    for (src_o, src_l), (dst_o, dst_l), d in zip(other_refs, pos_refs, others):
        for rr in range(d):
            dst_o[pl.ds(rr, tq // d, stride=d), :] = src_o[rr]
            dst_l[pl.ds(rr, tq // d, stride=d), :] = src_l[rr]

    rb = q_ref.shape[0]
    subs = [(rr, sb) for rr in range(rb) for sb in range(tq // A_SUB)]
    row_sl = [slice(sb * A_SUB, (sb + 1) * A_SUB) for _, sb in subs]
    m0s = [i * tq + sb * A_SUB for _, sb in subs]
    kss = [pl.multiple_of(jnp.clip(m0 - REACH, 0, n - A_KW), REACH) for m0 in m0s]
    sels = [jnp.where(m0 == 0, 1, jnp.where(m0 == n - A_SUB, 2, 0)) for m0 in m0s]
    ss = [lax.dot_general(q_ref[rr, rows, :], k_ref[rr, pl.ds(ks, A_KW), :], (((1,), (1,)), ((), ())),
                          preferred_element_type=F32) for (rr, _), rows, ks in zip(subs, row_sl, kss)]
    ss = [s * (1.0 / math.sqrt(A_HEAD_DIM)) + bm_ref[sel] for s, sel in zip(ss, sels)]
    ms = [jnp.max(s, axis=-1, keepdims=True) for s in ss]
    ps = [jnp.exp(s - m) for s, m in zip(ss, ms)]
    ls = [jnp.sum(p, axis=-1, keepdims=True) for p in ps]
    pvs = [jnp.dot(p.astype(BF16), v_ref[rr, pl.ds(ks, A_KW), :], preferred_element_type=F32)
           for (rr, _), p, ks in zip(subs, ps, kss)]
    for (rr, _), rows, m, l, pv in zip(subs, row_sl, ms, ls, pvs):
        o = pv / l
        lse = jnp.broadcast_to(m + jnp.log(l), o.shape)
        if others:
            lses = [lse] + [dst_l[rows, :] for _, dst_l in pos_refs]
            outs = [o] + [dst_o[rows, :] for dst_o, _ in pos_refs]
            mx = functools.reduce(jnp.maximum, lses)
            ws = [jnp.exp(x - mx) for x in lses]
            o = sum(w * x for w, x in zip(ws, outs)) / sum(ws)
            o_ref[rows, :] = (o * _silu(gate_ref[rows, :])).astype(o_ref.dtype)
        else:
            o_ref[rr, rows, :] = o
            lse_ref[rr, rows, :] = lse


def _dilated_group(tab, qkv, g, others, gate_src, tq=512):
    win, dil = DIL_PAIRS[g]
    assert win // (2 * dil) == REACH
    _, _, n, hd = qkv.shape
    tq = min(tq, n)
    rb = min(dil, A_CHAINS * A_SUB // tq)
    assert qkv.shape[1] == dil and n % tq == 0 and n >= A_KW and dil % rb == 0
    last = others is not None
    qblk = pl.BlockSpec((None, rb, tq, hd), lambda h, r, i: (h, r, i, 0))
    in_specs = [pl.BlockSpec(memory_space=pltpu.SMEM), qblk,
                pl.BlockSpec((None, rb, n, hd), lambda h, r, i: (A_HEADS + h, r, 0, 0)),
                pl.BlockSpec((None, rb, n, hd), lambda h, r, i: (2 * A_HEADS + h, r, 0, 0))]
    args = [tab, qkv, qkv, qkv]
    scratch = [pltpu.VMEM((3, A_SUB, A_KW), F32)]
    other_dils = ()
    if last:
        assert dil == 1
        other_dils = tuple(o.shape[1] for o, _ in others)
        for (o, lse), d in zip(others, other_dils):
            spec = pl.BlockSpec((None, d, tq // d, hd), lambda h, r, i: (h, 0, i, 0))
            in_specs += [spec, spec]
            args += [o, lse]
            scratch += [pltpu.VMEM((tq, hd), F32), pltpu.VMEM((tq, hd), F32)]
        in_specs.append(pl.BlockSpec((tq, hd), lambda h, r, i: (i, h)))
        args.append(gate_src)
        out_shape = jax.ShapeDtypeStruct((n, A_HEADS * hd), BF16)
        out_specs = pl.BlockSpec((tq, hd), lambda h, r, i: (i, h))
    else:
        out_shape = (jax.ShapeDtypeStruct((A_HEADS, dil, n, hd), F32),) * 2
        out_specs = (qblk, qblk)
    return pl.pallas_call(
        functools.partial(_dil_kernel, dil=dil, col0=g * A_HEADS, others=other_dils, n=n, tq=tq),
        grid=(A_HEADS, dil // rb, n // tq),
        in_specs=in_specs,
        out_specs=out_specs,
        out_shape=out_shape,
        scratch_shapes=scratch,
        compiler_params=_params("arbitrary", "arbitrary", "arbitrary"),
        name=f"dilated_attn_g{g}",
    )(*args)


def _strip_kernel(tab_ref, o_ref, *, dmin, rows):
    h, r = pl.program_id(0), pl.program_id(1)
    width = o_ref.shape[1]
    col = 3 * A_HEADS + h
    far_lo = tab_ref[N_BUCKETS // 2 - 1, col] * LOG2E
    far_hi = tab_ref[N_BUCKETS - 1, col] * LOG2E
    r0 = r * rows
    band = pl.cdiv(rows + 2 * FAR_DIST, LANES) * LANES + LANES
    lo = pl.multiple_of(jnp.clip((r0 - dmin - FAR_DIST) // LANES * LANES, 0, width - band), LANES)
    c_all = lax.broadcasted_iota(jnp.int32, (rows, width), 1)
    t_all = r0 + lax.broadcasted_iota(jnp.int32, (rows, width), 0)
    o_ref[...] = jnp.where(c_all - t_all + dmin < 0, far_lo, far_hi)
    t = r0 + lax.broadcasted_iota(jnp.int32, (rows, band), 0)
    c = lo + lax.broadcasted_iota(jnp.int32, (rows, band), 1)
    o_ref[:, pl.ds(lo, band)] = _rel_bias(c - t + dmin, tab_ref, col) * LOG2E


def _bias_strips(tab, tq, dmin, width, rows=64):
    assert width >= pl.cdiv(rows + 2 * FAR_DIST, LANES) * LANES + LANES
    return pl.pallas_call(
        functools.partial(_strip_kernel, dmin=dmin, rows=rows),
        grid=(C_HEADS, tq // rows),
        in_specs=[pl.BlockSpec(memory_space=pltpu.SMEM)],
        out_specs=pl.BlockSpec((None, rows, width), lambda h, r: (h, r, 0)),
        out_shape=jax.ShapeDtypeStruct((C_HEADS, tq, width), F32),
        compiler_params=_params("parallel", "parallel"),
        name="bias_strips",
    )(tab)


def _diff_kernel(strip_ref, dl_ref, g_ref, q_ref, k_ref, v_ref, gate_ref, o_ref, v2_ref, *,
                 lam_init, S, tq, tk, nsub, unroll, dmin, dmax):
    i = pl.program_id(1)
    nk = S // tk
    tks = tk // nsub

    @pl.when(i == 0)
    def _():
        v2_ref[:, :C_V_DIM] = v_ref[...]
        v2_ref[:, C_V_DIM:] = jnp.ones((S, C_V_DIM), BF16)

    q = q_ref[...]
    lane = lax.broadcasted_iota(jnp.int32, q.shape, 1)
    zero = jnp.zeros_like(q)
    qs = (jnp.where(lane < C_QK_DIM, q, zero), jnp.where(lane >= C_QK_DIM, q, zero))

    def logits(j, u):
        k0 = pl.multiple_of(j * tk, tk)
        b0 = pl.multiple_of(jnp.clip(j * tk - i * tq, dmin, dmax) - dmin, LANES)
        kt = k_ref[pl.ds(k0 + u * tks, tks), :]
        b = strip_ref[:, pl.ds(b0 + u * tks, tks)]
        return [lax.dot_general(qc, kt, (((1,), (1,)), ((), ())), preferred_element_type=F32) + b for qc in qs]

    def values(j, u):
        return v2_ref[pl.ds(pl.multiple_of(j * tk, tk) + u * tks, tks), :]

    def key_tiles(first, carry):
        steps = [(first * unroll + jj, u) for jj in range(unroll) for u in range(nsub)]
        ahead = logits(*steps[0])
        for idx, (j, u) in enumerate(steps):
            ss, ahead = ahead, (logits(*steps[idx + 1]) if idx + 1 < len(steps) else None)
            ms = [jnp.maximum(m, jnp.max(s, axis=-1, keepdims=True)) for s, (m, _) in zip(ss, carry)]
            ps = [jnp.exp2((s - m_new).astype(BF16)) for s, m_new in zip(ss, ms)]
            pvs = [jnp.dot(p, values(j, u), preferred_element_type=F32) for p in ps]
            carry = tuple((m_new, jnp.exp2(m - m_new) * acc + pv) for m_new, pv, (m, acc) in zip(ms, pvs, carry))
        return carry

    start = (jnp.full((tq, 1), NEG, F32), jnp.zeros((tq, 2 * C_V_DIM), F32))
    if nk == unroll:
        carry = key_tiles(0, (start, start))
    else:
        carry = lax.fori_loop(0, nk // unroll, key_tiles, (start, start))
    on = [acc[:, :C_V_DIM] / acc[:, C_V_DIM:] for _, acc in carry]
    dl = dl_ref[...]
    lam = (jnp.exp(jnp.sum(dl[0:1] * dl[1:2], axis=-1, keepdims=True))
           - jnp.exp(jnp.sum(dl[2:3] * dl[3:4], axis=-1, keepdims=True)) + lam_init)
    o = on[0] - lam * on[1]
    y = o * lax.rsqrt(jnp.mean(o * o, axis=-1, keepdims=True) + NORM_EPS) * g_ref[...]
    o_ref[...] = (y * (1.0 - lam_init) * _silu(gate_ref[...])).astype(o_ref.dtype)


C_TQ, C_TK, C_NSUB = 512, 2048, 4


def _strip_geometry(S):
    tq, tk = min(C_TQ, S), min(C_TK, S)
    dmax = pl.cdiv(FAR_DIST + tq - 1, LANES) * LANES
    dmin = -pl.cdiv(FAR_DIST + tk - 1, LANES) * LANES
    return tq, tk, dmin, dmax, dmax - dmin + tk


def _diff_attention(strips, cqkv, diff_lam, diff_g, layer, gate_src, gate_col):
    _, _, S, hd = cqkv.shape
    tq, tk, dmin, dmax, width = _strip_geometry(S)
    nsub = C_NSUB
    assert S % tk == 0 and S % tq == 0 and strips.shape == (C_HEADS, tq, width)
    lam_init = 0.8 - 0.6 * math.exp(-0.3 * layer)
    gcb = gate_col // hd
    return pl.pallas_call(
        functools.partial(_diff_kernel, lam_init=lam_init, S=S, tq=tq, tk=tk, nsub=nsub,
                          unroll=math.gcd(S // tk, 4), dmin=dmin, dmax=dmax),
        grid=(C_HEADS, S // tq),
        in_specs=[pl.BlockSpec((None, tq, width), lambda h, i: (h, 0, 0)),
                  pl.BlockSpec((None, 4, C_QK_DIM), lambda h, i: (layer, 0, 0)),
                  pl.BlockSpec((None, 1, C_V_DIM), lambda h, i: (layer, 0, 0)),
                  pl.BlockSpec((None, None, tq, hd), lambda h, i: (h, 0, i, 0)),
                  pl.BlockSpec((None, None, S, hd), lambda h, i: (C_HEADS + h, 0, 0, 0)),
                  pl.BlockSpec((None, None, S, hd), lambda h, i: (2 * C_HEADS + h, 0, 0, 0)),
                  pl.BlockSpec((tq, hd), lambda h, i: (i, gcb + h))],
        out_specs=pl.BlockSpec((tq, hd), lambda h, i: (i, h)),
        out_shape=jax.ShapeDtypeStruct((S, C_HEADS * hd), BF16),
        scratch_shapes=[pltpu.VMEM((S, 2 * C_V_DIM), BF16)],
        compiler_params=_params("arbitrary", "arbitrary"),
        name="diff_attn",
    )(strips, diff_lam, diff_g.reshape(diff_g.shape[0], 1, C_V_DIM), cqkv, cqkv, cqkv, gate_src)


def _short_conv_kernel(b_ref, w_ref, o_ref):
    b = b_ref[...]
    S = b.shape[0]
    row = lax.broadcasted_iota(jnp.int32, b.shape, 0)
    prev = jnp.where(row == 0, 0.0, pltpu.roll(b, 1, 0))
    nxt = jnp.where(row == S - 1, 0.0, pltpu.roll(b, S - 1, 0))
    w = w_ref[...]
    o_ref[...] = w[0:1] * prev + w[1:2] * b + w[2:3] * nxt


def _short_conv(p1, hy_conv, layer, col0):
    S = p1.shape[0]
    nct = BRANCH_W // LANES
    cb = col0 // LANES
    return pl.pallas_call(
        _short_conv_kernel,
        grid=(3 * nct,),
        in_specs=[pl.BlockSpec((S, LANES), lambda c: (0, cb + c)),
                  pl.BlockSpec((None, 3, LANES), lambda c: (layer, 0, c))],
        out_specs=pl.BlockSpec((None, S, LANES), lambda c: (c // nct, 0, c % nct)),
        out_shape=jax.ShapeDtypeStruct((3, S, BRANCH_W), F32),
        compiler_params=_params("parallel"),
        name="short_conv",
    )(p1, hy_conv)


def _filter_dft_kernel(w1t_ref, w1c_ref, w1s_ref, b1_ref, fr_ref, w2_ref, b2_ref, w30f_ref, w30b_ref, w31f_ref,
                       w31b_ref, skip_ref, ad_ref, fa_ref, o_ref, ys_ref, *, L):
    j = pl.program_id(0)
    N = 2 * L
    N1 = fa_ref.shape[1] // 3
    N2 = N // N1
    M = fa_ref.shape[0]
    nch = o_ref.shape[-1] // LANES

    def hidden(pos):
        t = pos * (1.0 / (L - 1))
        w = pos * (2.0 * math.pi / L)
        fb = (1e-4 + lax.broadcasted_iota(jnp.int32, (HY_BANDS, 1), 0).astype(F32)
              * ((HY_BANDS - 1 - 1e-4) / (HY_BANDS - 1)))
        ang = fb * w
        pre = (w1t_ref[...] * t
               + jnp.dot(w1c_ref[...], jnp.cos(ang), precision=HIGHEST, preferred_element_type=F32)
               - jnp.dot(w1s_ref[...], jnp.sin(ang), precision=HIGHEST, preferred_element_type=F32)
               + b1_ref[...])
        fr = fr_ref[...]
        hid = jnp.sin(fr[:, 0:1] * pre)
        hid = jnp.sin(fr[:, 1:2] * (jnp.dot(w2_ref[...], hid, precision=HIGHEST, preferred_element_type=F32)
                                    + b2_ref[...]))
        return hid.T

    def split_rows(x):
        hi, lo = _split_bf16(x)
        return jnp.concatenate([hi, lo, hi], axis=0)

    w3 = [[split_rows(ref[...]) for ref in pair] for pair in ((w30f_ref, w30b_ref), (w31f_ref, w31b_ref))]
    hid0 = _const3(jnp.broadcast_to(hidden(jnp.zeros((1, LANES), F32))[0:1], (SUBLANES, HY_FFN)))
    lag0 = [jnp.dot(hid0, w3[o][1], preferred_element_type=F32)[0:1] + skip_ref[o] for o in range(HY_ORDER)]
    fa3 = fa_ref[...]
    half = N1 // 2
    lag = lambda r: jnp.where(r < L, r, N - r).astype(F32)
    lane = lax.broadcasted_iota(jnp.int32, (1, SUBLANES * N1), 1)
    hid_all = hidden(lag((lane % N1) * N2 + j * SUBLANES + lane // N1))
    rows = [lax.broadcasted_iota(jnp.int32, (N1, 1), 0) * N2 + j * SUBLANES + s for s in range(SUBLANES)]
    hids = [(_const3(hid_all[s * N1:s * N1 + half]), _const3(hid_all[s * N1 + half:(s + 1) * N1]))
            for s in range(SUBLANES)]
    decays = [jnp.exp(-(lag(row) * (1.0 / (L - 1))) * ad_ref[...]) for row in rows]
    pairs = [(s, o) for s in range(SUBLANES) for o in range(HY_ORDER)]
    gs = [jnp.concatenate([jnp.dot(hids[s][0], w3[o][0], preferred_element_type=F32),
                           jnp.dot(hids[s][1], w3[o][1], preferred_element_type=F32)], axis=0) * decays[s]
          for s, o in pairs]
    gs = [jnp.where(rows[0] == L, 0.0, g + jnp.where(rows[0] == 0, lag0[o], 0.0)) if s == 0 else g
          for g, (s, o) in zip(gs, pairs)]
    ys = [_dot3(fa3, g) for g in gs]
    for y, (s, o) in zip(ys, pairs):
        for c in range(nch):
            ys_ref[o, c, pl.ds(s, M, stride=SUBLANES), :] = y[:, c * LANES:(c + 1) * LANES]
    for o in range(HY_ORDER):
        for c in range(nch):
            o_ref[o, :, :, c * LANES:(c + 1) * LANES] = ys_ref[o, c].reshape(M, SUBLANES, LANES)


def _filter_outer_dft(L, fa3, hy_w1, hy_b1, hy_freq, hy_w2, hy_b2, hy_w3, hy_skip, layer):
    assert HY_ORDER == 2
    N = 2 * L
    W = BRANCH_W
    nb = HY_BANDS
    M, N1 = fa3.shape[0], fa3.shape[1] // 3
    N2 = N // N1
    w1 = hy_w1[layer].T
    max_decay = math.log(HY_TARGET) / HY_FAST_PCT
    min_decay = math.log(HY_TARGET) / HY_SLOW_PCT
    absdelta = jnp.abs(jnp.linspace(min_decay, max_decay, W, dtype=F32)).reshape(1, W)
    full = lambda shape: pl.BlockSpec(shape, lambda j: (0,) * len(shape))
    w3_spec = lambda o, bwd: pl.BlockSpec((None, HY_FFN, W), lambda j: (layer, 0, 2 * o + bwd))
    return pl.pallas_call(
        functools.partial(_filter_dft_kernel, L=L),
        grid=(N2 // SUBLANES,),
        in_specs=[full((HY_FFN, 1)), full((HY_FFN, nb)), full((HY_FFN, nb)), full((HY_FFN, 1)), full((HY_FFN, 2)),
                  full((HY_FFN, HY_FFN)), full((HY_FFN, 1)),
                  w3_spec(0, 0), w3_spec(0, 1), w3_spec(1, 0), w3_spec(1, 1),
                  pl.BlockSpec((None, HY_ORDER, 1, W), lambda j: (layer, 0, 0, 0)),
                  full((1, W)), full((M, 3 * N1))],
        out_specs=pl.BlockSpec((HY_ORDER, M, SUBLANES, W), lambda j: (0, 0, j, 0)),
        out_shape=jax.ShapeDtypeStruct((HY_ORDER, M, N2, W), F32),
        scratch_shapes=[pltpu.VMEM((HY_ORDER, W // LANES, M * SUBLANES, LANES), F32)],
        compiler_params=_params("arbitrary"),
        name="filter_dft",
    )(w1[:, 0:1], w1[:, 1:1 + nb], w1[:, 1 + nb:], hy_b1[layer].reshape(HY_FFN, 1), hy_freq[layer].T,
      hy_w2[layer].T, hy_b2[layer].reshape(HY_FFN, 1), hy_w3, hy_w3, hy_w3, hy_w3,
      hy_skip.reshape(hy_skip.shape[0], HY_ORDER, 1, W), absdelta, fa3)


def _outer_dft_kernel(f_ref, x_ref, *rest, gated):
    gate_ref = rest[0] if gated else None
    o_ref, xs_ref, ys_ref = rest[-3:]
    K, sub, tw = x_ref.shape
    M = o_ref.shape[0]
    nch = tw // LANES
    for c in range(nch):
        xs_ref[c] = x_ref[:, :, c * LANES:(c + 1) * LANES].reshape(K * sub, LANES)
    f3 = f_ref[...]
    for s in range(sub):
        x = jnp.concatenate([xs_ref[c, pl.ds(s, K, stride=sub), :] for c in range(nch)], axis=1)
        y = _dot3(f3, x)
        for c in range(nch):
            ys_ref[c, pl.ds(s, M, stride=sub), :] = y[:, c * LANES:(c + 1) * LANES]
    for c in range(nch):
        y = ys_ref[c].reshape(M, sub, LANES)
        if gated:
            y = y * gate_ref[:, :, c * LANES:(c + 1) * LANES]
        o_ref[:, :, c * LANES:(c + 1) * LANES] = y


def _outer_dft(f3, x, xb, gate=None, gb=0, tw=BRANCH_W):
    _, K, N2, W = x.shape
    M = f3.shape[0]
    B = x.shape[0] if xb is None else 1
    x_idx = (lambda b: b) if xb is None else (lambda b: xb)
    blk = lambda rows, idx: pl.BlockSpec((None, rows, SUBLANES, tw), lambda b, t, c: (idx(b), 0, t, c))
    in_specs = [pl.BlockSpec((M, 3 * K), lambda b, t, c: (0, 0)), blk(K, x_idx)]
    args = [f3, x]
    if gate is not None:
        in_specs.append(blk(M, lambda b: gb))
        args.append(gate)
    return pl.pallas_call(
        functools.partial(_outer_dft_kernel, gated=gate is not None),
        grid=(B, N2 // SUBLANES, W // tw),
        in_specs=in_specs,
        out_specs=blk(M, lambda b: b),
        out_shape=jax.ShapeDtypeStruct((B, M, N2, W), F32),
        scratch_shapes=[pltpu.VMEM((tw // LANES, K * SUBLANES, LANES), F32),
                        pltpu.VMEM((tw // LANES, M * SUBLANES, LANES), F32)],
        compiler_params=_params("parallel", "parallel", "parallel"),
        name="dft_outer",
    )(*args)


def _conv_mid_kernel(gf_ref, gi_ref, a_ref, f_ref, o_ref, *, live):
    k = pl.program_id(0)

    @pl.when(k < live)
    def _():
        half = a_ref.shape[0] // 2
        gf3 = gf_ref[...]
        x = _dot3(gf3, a_ref[...])
        h = _dot3(gf3, f_ref[...])
        xr, xi = x[:half], x[half:]
        hr, hi = h[:half], h[half:]
        y = jnp.concatenate([xr * hr - xi * hi, xr * hi + xi * hr], axis=0)
        o_ref[...] = _dot3(gi_ref[...], y)

    @pl.when(k >= live)
    def _():
        o_ref[...] = jnp.zeros(o_ref.shape, F32)


def _conv_mid(gf3, gi3, a, hspec, order, live):
    NS, R, W = a.shape
    src = lambda k: jnp.minimum(k, live - 1)
    return pl.pallas_call(
        functools.partial(_conv_mid_kernel, live=live),
        grid=(NS,),
        in_specs=[pl.BlockSpec((None, R, 3 * R), lambda k: (src(k), 0, 0)),
                  pl.BlockSpec((None, R, 3 * R), lambda k: (src(k), 0, 0)),
                  pl.BlockSpec((None, R, W), lambda k: (src(k), 0, 0)),
                  pl.BlockSpec((None, None, R, W), lambda k: (order, src(k), 0, 0))],
        out_specs=pl.BlockSpec((None, R, W), lambda k: (k, 0, 0)),
        out_shape=jax.ShapeDtypeStruct(a.shape, F32),
        compiler_params=_params("parallel"),
        name="conv_mid",
    )(gf3, gi3, a, hspec)


def _dft_tables(L):
    N = 2 * L
    N2 = LANES
    N1 = N // N2
    NS = pl.cdiv(N1 // 2 + 1, SUBLANES) * SUBLANES
    two_pi = 2.0 * math.pi
    ks = jnp.arange(NS, dtype=jnp.int32)
    t1 = jnp.arange(N1, dtype=jnp.int32)
    live = (ks <= N1 // 2).astype(F32)
    ang = ((ks[:, None] * t1[None, :]) % N1).astype(F32) * (two_pi / N1)
    cs = jnp.stack([jnp.cos(ang), -jnp.sin(ang)], axis=1) * live[:, None, None]
    fa = cs.reshape(2 * NS, N1)
    weight = jnp.where((ks == 0) | (ks == N1 // 2), 1.0, 2.0) / N
    fi = jnp.transpose(cs * weight[:, None, None], (2, 0, 1)).reshape(N1, 2 * NS)[:N1 // 2]
    t2 = jnp.arange(N2, dtype=jnp.int32)
    k = ks[:, None, None] + N1 * t2[None, :, None]
    ph = ((k * t2[None, None, :]) % N).astype(F32) * (two_pi / N)
    gr, gim = jnp.cos(ph), -jnp.sin(ph)
    gf = jnp.concatenate([jnp.concatenate([gr, -gim], axis=2), jnp.concatenate([gim, gr], axis=2)], axis=1)
    gi = jnp.swapaxes(gf, 1, 2)
    return {"fa": _const3(fa), "fh": _const3(fa[:, :N1 // 2]), "fi": _const3(fi), "gf": _const3(gf),
            "gi": _const3(gi), "N1": N1, "N2": N2, "NS": NS}


def _long_conv(z, zi, gate, gi_, hspec, order, tb):
    _, S, W = z.shape
    N1, N2, NS = tb["N1"], tb["N2"], tb["NS"]
    split = lambda t: t.reshape(t.shape[0], N1 // 2, N2, W)
    a = _outer_dft(tb["fh"], split(z), zi)
    b = _conv_mid(tb["gf"], tb["gi"], a.reshape(NS, 2 * N2, W), hspec, order, N1 // 2 + 1)
    y = _outer_dft(tb["fi"], b.reshape(1, 2 * NS, N2, W), 0, split(gate), gi_)
    return y.reshape(1, S, W)


def _merge_kernel(a_ref, zb_ref, bg_ref, c_ref, w_ref, ma_ref, mb_ref, mc_ref, bias_ref, o_ref):
    b = (zb_ref[...] * _silu(bg_ref[...])).astype(BF16)
    bias = bias_ref[...]
    y = None
    for n, (br, m_ref) in enumerate(((a_ref[...], ma_ref), (b, mb_ref), (c_ref[...], mc_ref))):
        pb = jnp.dot(br, w_ref[n], preferred_element_type=F32)
        gate = _sigmoid(m_ref[...] + bias[n:n + 1])
        y = gate * pb if y is None else y + gate * pb
    o_ref[...] = y.astype(o_ref.dtype)


def _merge(a_out, zb, p1, c_out, mg, w_proj_bf, merge_b, layer, tm=256):
    S = a_out.shape[0]
    W, D = BRANCH_W, D_MODEL
    bgb = (P1_COLS - W) // W
    branch = pl.BlockSpec((tm, W), lambda i: (i, 0))
    mspec = lambda n: pl.BlockSpec((tm, D), lambda i: (i, n))
    return pl.pallas_call(
        _merge_kernel,
        grid=(S // tm,),
        in_specs=[branch, pl.BlockSpec((None, tm, W), lambda i: (0, i, 0)),
                  pl.BlockSpec((tm, W), lambda i: (i, bgb)), branch,
                  pl.BlockSpec((None, N_BRANCH, W, D), lambda i: (layer, 0, 0, 0)),
                  mspec(0), mspec(1), mspec(2),
                  pl.BlockSpec((None, N_BRANCH, D), lambda i: (layer, 0, 0))],
        out_specs=pl.BlockSpec((tm, D), lambda i: (i, 0)),
        out_shape=jax.ShapeDtypeStruct((S, D), BF16),
        compiler_params=_params("parallel"),
        name="merge",
    )(a_out, zb, p1, c_out, w_proj_bf, mg, mg, mg, merge_b)


def _out_kernel(x_ref, y_ref, w_ref, g_ref, xo_ref, ho_ref):
    x = x_ref[...] + jnp.dot(y_ref[...], w_ref[...], preferred_element_type=F32)
    xo_ref[...] = x
    ms = jnp.mean(x * x, axis=-1, keepdims=True)
    ho_ref[...] = (x * lax.rsqrt(ms + NORM_EPS) * g_ref[...]).astype(ho_ref.dtype)


def _out_proj(x, y, w_out_bf, layer, g, h_dtype, tm=512):
    S, D = x.shape
    row = pl.BlockSpec((tm, D), lambda i: (i, 0))
    return pl.pallas_call(
        _out_kernel,
        grid=(S // tm,),
        in_specs=[row, row, pl.BlockSpec((None, D, D), lambda i: (layer, 0, 0)), pl.BlockSpec((1, D), lambda i: (0, 0))],
        out_specs=(row, row),
        out_shape=(jax.ShapeDtypeStruct((S, D), F32), jax.ShapeDtypeStruct((S, D), h_dtype)),
        compiler_params=_params("parallel"),
        name="out_proj",
    )(x, y, w_out_bf, g.reshape(1, D))


def _encode(xb, norm_g, final_g, w_in, merge_b, rel_bias, hy_conv, hy_w1, hy_b1, hy_freq, hy_w2, hy_b2, hy_w3,
            hy_skip, diff_lam, diff_g, w_proj_bf, w_out_bf, tb):
    S = xb.shape[0]
    depth = w_in.shape[0]
    N1, N2 = tb["N1"], tb["N2"]
    h = _rmsnorm(xb, norm_g[0], BF16)
    for l in range(depth):
        qkv = [_in_proj(h, w_in, l, COL_A + g * A_GROUP_COLS, A_GROUP_COLS, dil=DIL_PAIRS[g][1]) for g in range(3)]
        p1 = _in_proj(h, w_in, l, COL_P1, P1_COLS)
        cqkv = _in_proj(h, w_in, l, COL_C, C_QKV_COLS, dil=1, scale0=C_Q_SCALE)
        cg = _in_proj(h, w_in, l, COL_P2, BRANCH_W)
        mg = _in_proj(h, w_in, l, COL_P2 + BRANCH_W, N_BRANCH * D_MODEL)

        others = [_dilated_group(rel_bias, qkv[g], g, None, None) for g in (1, 2)]
        a_out = _dilated_group(rel_bias, qkv[0], 0, others, p1, tq=A_CHAINS * A_SUB)

        u = _short_conv(p1, hy_conv, l, BRANCH_W)
        fa_out = _filter_outer_dft(S, tb["fa"], hy_w1, hy_b1, hy_freq, hy_w2, hy_b2, hy_w3, hy_skip, l)
        hspec = fa_out.reshape(HY_ORDER, tb["NS"], 2 * N2, BRANCH_W)
        z = _long_conv(u, 0, u, 1, hspec, 0, tb)
        zb = _long_conv(z, 0, u, 2, hspec, 1, tb)

        c_out = _diff_attention(tb["strips"], cqkv, diff_lam, diff_g, l, cg, 0)

        y = _merge(a_out, zb, p1, c_out, mg, w_proj_bf, merge_b, l)
        if l + 1 < depth:
            xb, h = _out_proj(xb, y, w_out_bf, l, norm_g[l + 1], BF16)
        else:
            _, h = _out_proj(xb, y, w_out_bf, l, final_g, F32)
    return h


def kernel(x, norm_g, final_g, w_in, merge_b, rel_bias, hy_conv, hy_w1, hy_b1, hy_freq, hy_w2, hy_b2, hy_w3,
           hy_skip, diff_lam, diff_g, w_proj, w_out):
    B, S, D = x.shape
    w_proj_bf = w_proj.astype(BF16)
    w_out_bf = w_out.astype(BF16)
    tb = _dft_tables(S)
    tq, _, dmin, _, width = _strip_geometry(S)
    tb["strips"] = _bias_strips(rel_bias, tq, dmin, width)
    outs = [_encode(x[b], norm_g, final_g, w_in, merge_b, rel_bias, hy_conv, hy_w1, hy_b1, hy_freq, hy_w2, hy_b2,
                    hy_w3, hy_skip, diff_lam, diff_g, w_proj_bf, w_out_bf, tb) for b in range(B)]
    return outs[0].reshape(1, S, D) if B == 1 else jnp.stack(outs)
```

```python
import functools
import math

import jax
import jax.numpy as jnp
from jax import lax
from jax.experimental import pallas as pl
from jax.experimental.pallas import tpu as pltpu

F32 = jnp.float32
BF16 = jnp.bfloat16

LANES = 128
SUBLANES = 8
D_MODEL = 2048
BRANCH_W = 1024
N_BRANCH = 3
DIL_PAIRS = ((128, 1), (512, 4), (2048, 16))
A_HEADS = 8
A_HEAD_DIM = 128
REACH = 64
A_SUB = 128
A_KW = A_SUB + 2 * REACH
A_CHAINS = 16
C_HEADS = 8
C_QK_DIM = 64
C_V_DIM = 128
HY_BANDS = 16
HY_FFN = 64
HY_ORDER = 2
HY_TARGET = 1e-2
HY_FAST_PCT = 0.3
HY_SLOW_PCT = 1.5
N_BUCKETS = 32
REL_MAX_DIST = 1024
NORM_EPS = 1e-6
NEG = -1e30

A_GROUP_COLS = 3 * A_HEADS * A_HEAD_DIM
A_QKV_COLS = 3 * A_GROUP_COLS
P1_COLS = BRANCH_W + 3 * BRANCH_W + BRANCH_W
C_QKV_COLS = 3 * C_HEADS * C_V_DIM
COL_A, COL_P1, COL_C, COL_P2 = 0, A_QKV_COLS, A_QKV_COLS + P1_COLS, A_QKV_COLS + P1_COLS + C_QKV_COLS

VMEM_LIMIT = 56 * 1024 * 1024
HIGHEST = lax.Precision.HIGHEST
LOG2E = math.log2(math.e)
C_Q_SCALE = LOG2E / math.sqrt(C_QK_DIM)

def _bucket_thresholds():
    nb = N_BUCKETS // 2
    max_exact = nb // 2
    thr = []
    for k in range(1, nb - max_exact):
        n = max_exact
        while int(math.log(n / max_exact) / math.log(REL_MAX_DIST / max_exact) * (nb - max_exact)) < k:
            n += 1
        thr.append(n)
    return tuple(thr)


BUCKET_THR = _bucket_thresholds()
FAR_DIST = BUCKET_THR[-1]


def _params(*sem):
    return pltpu.CompilerParams(dimension_semantics=sem, vmem_limit_bytes=VMEM_LIMIT)


def _rel_bias(rel, tab_ref, col):
    nb = N_BUCKETS // 2
    n = jnp.abs(rel)
    large = jnp.full(rel.shape, nb // 2, jnp.int32)
    for thr in BUCKET_THR:
        large = large + (n >= thr).astype(jnp.int32)
    bucket = jnp.where(n < nb // 2, n, large) + jnp.where(rel > 0, nb, 0)
    out = jnp.full(rel.shape, tab_ref[0, col], F32)
    for j in range(1, N_BUCKETS):
        out = jnp.where(bucket == j, tab_ref[j, col], out)
    return out


def _sigmoid(x):
    return 0.5 * jnp.tanh(0.5 * x) + 0.5


def _silu(x):
    return x * _sigmoid(x)


def _split_bf16(x):
    hi = x.astype(BF16)
    return hi, (x - hi.astype(F32)).astype(BF16)


def _const3(a):
    hi, lo = _split_bf16(a)
    return jnp.concatenate([hi, hi, lo], axis=-1)


def _dot3(a3, x):
    hi, lo = _split_bf16(x)
    return jnp.dot(a3, jnp.concatenate([hi, lo, hi], axis=0), preferred_element_type=F32)


def _rmsnorm_kernel(x_ref, g_ref, o_ref):
    x = x_ref[...]
    ms = jnp.mean(x * x, axis=-1, keepdims=True)
    o_ref[...] = (x * lax.rsqrt(ms + NORM_EPS) * g_ref[...]).astype(o_ref.dtype)


def _rmsnorm(x, g, out_dtype, tm=512):
    S, D = x.shape
    return pl.pallas_call(
        _rmsnorm_kernel,
        grid=(S // tm,),
        in_specs=[pl.BlockSpec((tm, D), lambda i: (i, 0)), pl.BlockSpec((1, D), lambda i: (0, 0))],
        out_specs=pl.BlockSpec((tm, D), lambda i: (i, 0)),
        out_shape=jax.ShapeDtypeStruct((S, D), out_dtype),
        compiler_params=_params("parallel"),
        name="rmsnorm",
    )(x, g.reshape(1, D))


def _proj_kernel(h_ref, w_ref, o_ref, wb_ref, *acc_ref, dil, scale0, kchunks):
    def emit(acc):
        if scale0 is not None:
            acc = acc * jnp.where(pl.program_id(0) == 0, scale0, 1.0)
        tm, tn = acc.shape
        if dil is None:
            o_ref[...] = acc
        elif dil == 1:
            for c in range(tn // LANES):
                o_ref[c, 0] = acc[:, c * LANES:(c + 1) * LANES].astype(o_ref.dtype)
        else:
            for c in range(tn // LANES):
                acc_ref[0][c] = acc[:, c * LANES:(c + 1) * LANES]
                for r in range(dil):
                    o_ref[c, r] = acc_ref[0][c, pl.ds(r, tm // dil, stride=dil), :].astype(o_ref.dtype)

    @pl.when(pl.program_id(1) == 0)
    def _():
        kc = w_ref.shape[0] // kchunks
        acc = None
        for c in range(kchunks):
            rows = slice(c * kc, (c + 1) * kc)
            wc = w_ref[rows, :].astype(BF16)
            wb_ref[rows, :] = wc
            part = jnp.dot(h_ref[:, rows], wc, preferred_element_type=F32)
            acc = part if acc is None else acc + part
        emit(acc)

    @pl.when(pl.program_id(1) > 0)
    def _():
        emit(jnp.dot(h_ref[...], wb_ref[...], preferred_element_type=F32))


def _in_proj(h, w_in, layer, col0, ncols, dil=None, scale0=None, tm=1024, tn=1024):
    S, D = h.shape
    assert col0 % tn == 0 and ncols % tn == 0
    jb = col0 // tn
    scratch = [pltpu.VMEM((D, tn), BF16)]
    if dil is None:
        out_shape = jax.ShapeDtypeStruct((S, ncols), F32)
        out_spec = pl.BlockSpec((tm, tn), lambda j, i: (i, j))
    else:
        out_shape = jax.ShapeDtypeStruct((ncols // LANES, dil, S // dil, LANES), BF16)
        out_spec = pl.BlockSpec((tn // LANES, dil, tm // dil, LANES), lambda j, i: (j, 0, i, 0))
        if dil > 1:
            scratch.append(pltpu.VMEM((tn // LANES, tm, LANES), F32))
    return pl.pallas_call(
        functools.partial(_proj_kernel, dil=dil, scale0=scale0, kchunks=4),
        grid=(ncols // tn, S // tm),
        in_specs=[pl.BlockSpec((tm, D), lambda j, i: (i, 0)),
                  pl.BlockSpec((None, D, tn), lambda j, i: (layer, 0, jb + j))],
        out_specs=out_spec,
        out_shape=out_shape,
        scratch_shapes=scratch,
        compiler_params=_params("parallel", "arbitrary"),
        name="in_proj",
    )(h, w_in)


def _dil_kernel(*refs, dil, col0, others, n, tq):
    it = iter(refs)
    tab_ref, q_ref, k_ref, v_ref = next(it), next(it), next(it), next(it)
    other_refs = [(next(it), next(it)) for _ in others]
    gate_ref = next(it) if others else None
    o_ref = next(it)
    lse_ref = None if others else next(it)
    bm_ref = next(it)
    pos_refs = [(next(it), next(it)) for _ in others]

    h, r, i = pl.program_id(0), pl.program_id(1), pl.program_id(2)
    deltas = (-REACH, 0, -2 * REACH)

    @pl.when((r == 0) & (i == 0))
    def _():
        t = lax.broadcasted_iota(jnp.int32, (A_SUB, A_KW), 0)
        u = lax.broadcasted_iota(jnp.int32, (A_SUB, A_KW), 1)
        for idx, delta in enumerate(deltas):
            rel = u + delta - t
            bias = _rel_bias(rel * dil, tab_ref, col0 + h)
            bm_ref[idx] = jnp.where(jnp.abs(rel) <= REACH, bias, NEG)

    for (src_o, src_l), (dst_o, dst_l), d in zip(other_refs, pos_refs, others):
        for rr in range(d):
            dst_o[pl.ds(rr, tq // d, stride=d), :] = src_o[rr]
            dst_l[pl.ds(rr, tq // d, stride=d), :] = src_l[rr]

    rb = q_ref.shape[0]
    subs = [(rr, sb) for rr in range(rb) for sb in range(tq // A_SUB)]
    row_sl = [slice(sb * A_SUB, (sb + 1) * A_SUB) for _, sb in subs]
    m0s = [i * tq + sb * A_SUB for _, sb in subs]
    kss = [pl.multiple_of(jnp.clip(m0 - REACH, 0, n - A_KW), REACH) for m0 in m0s]
    sels = [jnp.where(m0 == 0, 1, jnp.where(m0 == n - A_SUB, 2, 0)) for m0 in m0s]
    ss = [lax.dot_general(q_ref[rr, rows, :], k_ref[rr, pl.ds(ks, A_KW), :], (((1,), (1,)), ((), ())),
                          preferred_element_type=F32) for (rr, _), rows, ks in zip(subs, row_sl, kss)]
    ss = [s * (1.0 / math.sqrt(A_HEAD_DIM)) + bm_ref[sel] for s, sel in zip(ss, sels)]
    ms = [jnp.max(s, axis=-1, keepdims=True) for s in ss]
    ps = [jnp.exp(s - m) for s, m in zip(ss, ms)]
    ls = [jnp.sum(p, axis=-1, keepdims=True) for p in ps]
    pvs = [jnp.dot(p.astype(BF16), v_ref[rr, pl.ds(ks, A_KW), :], preferred_element_type=F32)
           for (rr, _), p, ks in zip(subs, ps, kss)]
    for (rr, _), rows, m, l, pv in zip(subs, row_sl, ms, ls, pvs):
        o = pv / l
        lse = jnp.broadcast_to(m + jnp.log(l), o.shape)
        if others:
            lses = [lse] + [dst_l[rows, :] for _, dst_l in pos_refs]
            outs = [o] + [dst_o[rows, :] for dst_o, _ in pos_refs]
            mx = functools.reduce(jnp.maximum, lses)
            ws = [jnp.exp(x - mx) for x in lses]
            o = sum(w * x for w, x in zip(ws, outs)) / sum(ws)
            o_ref[rows, :] = (o * _silu(gate_ref[rows, :])).astype(o_ref.dtype)
        else:
            o_ref[rr, rows, :] = o
            lse_ref[rr, rows, :] = lse


def _dilated_group(tab, qkv, g, others, gate_src, tq=512):
    win, dil = DIL_PAIRS[g]
    assert win // (2 * dil) == REACH
    _, _, n, hd = qkv.shape
    tq = min(tq, n)
    rb = min(dil, A_CHAINS * A_SUB // tq)
    assert qkv.shape[1] == dil and n % tq == 0 and n >= A_KW and dil % rb == 0
    last = others is not None
    qblk = pl.BlockSpec((None, rb, tq, hd), lambda h, r, i: (h, r, i, 0))
    in_specs = [pl.BlockSpec(memory_space=pltpu.SMEM), qblk,
                pl.BlockSpec((None, rb, n, hd), lambda h, r, i: (A_HEADS + h, r, 0, 0)),
                pl.BlockSpec((None, rb, n, hd), lambda h, r, i: (2 * A_HEADS + h, r, 0, 0))]
    args = [tab, qkv, qkv, qkv]
    scratch = [pltpu.VMEM((3, A_SUB, A_KW), F32)]
    other_dils = ()
    if last:
        assert dil == 1
        other_dils = tuple(o.shape[1] for o, _ in others)
        for (o, lse), d in zip(others, other_dils):
            spec = pl.BlockSpec((None, d, tq // d, hd), lambda h, r, i: (h, 0, i, 0))
            in_specs += [spec, spec]
            args += [o, lse]
            scratch += [pltpu.VMEM((tq, hd), F32), pltpu.VMEM((tq, hd), F32)]
        in_specs.append(pl.BlockSpec((tq, hd), lambda h, r, i: (i, h)))
        args.append(gate_src)
        out_shape = jax.ShapeDtypeStruct((n, A_HEADS * hd), BF16)
        out_specs = pl.BlockSpec((tq, hd), lambda h, r, i: (i, h))
    else:
        out_shape = (jax.ShapeDtypeStruct((A_HEADS, dil, n, hd), F32),) * 2
        out_specs = (qblk, qblk)
    return pl.pallas_call(
        functools.partial(_dil_kernel, dil=dil, col0=g * A_HEADS, others=other_dils, n=n, tq=tq),
        grid=(A_HEADS, dil // rb, n // tq),
        in_specs=in_specs,
        out_specs=out_specs,
        out_shape=out_shape,
        scratch_shapes=scratch,
        compiler_params=_params("arbitrary", "arbitrary", "arbitrary"),
        name=f"dilated_attn_g{g}",
    )(*args)


def _strip_kernel(tab_ref, o_ref, *, dmin, rows):
    h, r = pl.program_id(0), pl.program_id(1)
    width = o_ref.shape[1]
    col = 3 * A_HEADS + h
    far_lo = tab_ref[N_BUCKETS // 2 - 1, col] * LOG2E
    far_hi = tab_ref[N_BUCKETS - 1, col] * LOG2E
    r0 = r * rows
    band = pl.cdiv(rows + 2 * FAR_DIST, LANES) * LANES + LANES
    lo = pl.multiple_of(jnp.clip((r0 - dmin - FAR_DIST) // LANES * LANES, 0, width - band), LANES)
    c_all = lax.broadcasted_iota(jnp.int32, (rows, width), 1)
    t_all = r0 + lax.broadcasted_iota(jnp.int32, (rows, width), 0)
    o_ref[...] = jnp.where(c_all - t_all + dmin < 0, far_lo, far_hi)
    t = r0 + lax.broadcasted_iota(jnp.int32, (rows, band), 0)
    c = lo + lax.broadcasted_iota(jnp.int32, (rows, band), 1)
    o_ref[:, pl.ds(lo, band)] = _rel_bias(c - t + dmin, tab_ref, col) * LOG2E


def _bias_strips(tab, tq, dmin, width, rows=64):
    assert width >= pl.cdiv(rows + 2 * FAR_DIST, LANES) * LANES + LANES
    return pl.pallas_call(
        functools.partial(_strip_kernel, dmin=dmin, rows=rows),
        grid=(C_HEADS, tq // rows),
        in_specs=[pl.BlockSpec(memory_space=pltpu.SMEM)],
        out_specs=pl.BlockSpec((None, rows, width), lambda h, r: (h, r, 0)),
        out_shape=jax.ShapeDtypeStruct((C_HEADS, tq, width), F32),
        compiler_params=_params("parallel", "parallel"),
        name="bias_strips",
    )(tab)


def _diff_kernel(strip_ref, dl_ref, g_ref, q_ref, k_ref, v_ref, gate_ref, o_ref, v2_ref, *,
                 lam_init, S, tq, tk, nsub, unroll, dmin, dmax):
    i = pl.program_id(1)
    nk = S // tk
    tks = tk // nsub

    @pl.when(i == 0)
    def _():
        v2_ref[:, :C_V_DIM] = v_ref[...]
        v2_ref[:, C_V_DIM:] = jnp.ones((S, C_V_DIM), BF16)

    q = q_ref[...]
    lane = lax.broadcasted_iota(jnp.int32, q.shape, 1)
    zero = jnp.zeros_like(q)
    qs = (jnp.where(lane < C_QK_DIM, q, zero), jnp.where(lane >= C_QK_DIM, q, zero))

    def logits(j, u):
        k0 = pl.multiple_of(j * tk, tk)
        b0 = pl.multiple_of(jnp.clip(j * tk - i * tq, dmin, dmax) - dmin, LANES)
        kt = k_ref[pl.ds(k0 + u * tks, tks), :]
        b = strip_ref[:, pl.ds(b0 + u * tks, tks)]
        return [lax.dot_general(qc, kt, (((1,), (1,)), ((), ())), preferred_element_type=F32) + b for qc in qs]

    def values(j, u):
        return v2_ref[pl.ds(pl.multiple_of(j * tk, tk) + u * tks, tks), :]

    def key_tiles(first, carry):
        steps = [(first * unroll + jj, u) for jj in range(unroll) for u in range(nsub)]
        ahead = logits(*steps[0])
        for idx, (j, u) in enumerate(steps):
            ss, ahead = ahead, (logits(*steps[idx + 1]) if idx + 1 < len(steps) else None)
            ms = [jnp.maximum(m, jnp.max(s, axis=-1, keepdims=True)) for s, (m, _) in zip(ss, carry)]
            ps = [jnp.exp2((s - m_new).astype(BF16)) for s, m_new in zip(ss, ms)]
            pvs = [jnp.dot(p, values(j, u), preferred_element_type=F32) for p in ps]
            carry = tuple((m_new, jnp.exp2(m - m_new) * acc + pv) for m_new, pv, (m, acc) in zip(ms, pvs, carry))
        return carry

    start = (jnp.full((tq, 1), NEG, F32), jnp.zeros((tq, 2 * C_V_DIM), F32))
    if nk == unroll:
        carry = key_tiles(0, (start, start))
    else:
        carry = lax.fori_loop(0, nk // unroll, key_tiles, (start, start))
    on = [acc[:, :C_V_DIM] / acc[:, C_V_DIM:] for _, acc in carry]
    dl = dl_ref[...]
    lam = (jnp.exp(jnp.sum(dl[0:1] * dl[1:2], axis=-1, keepdims=True))
           - jnp.exp(jnp.sum(dl[2:3] * dl[3:4], axis=-1, keepdims=True)) + lam_init)
    o = on[0] - lam * on[1]
    y = o * lax.rsqrt(jnp.mean(o * o, axis=-1, keepdims=True) + NORM_EPS) * g_ref[...]
    o_ref[...] = (y * (1.0 - lam_init) * _silu(gate_ref[...])).astype(o_ref.dtype)


C_TQ, C_TK, C_NSUB = 512, 2048, 4


def _strip_geometry(S):
    tq, tk = min(C_TQ, S), min(C_TK, S)
    dmax = pl.cdiv(FAR_DIST + tq - 1, LANES) * LANES
    dmin = -pl.cdiv(FAR_DIST + tk - 1, LANES) * LANES
    return tq, tk, dmin, dmax, dmax - dmin + tk


def _diff_attention(strips, cqkv, diff_lam, diff_g, layer, gate_src, gate_col):
    _, _, S, hd = cqkv.shape
    tq, tk, dmin, dmax, width = _strip_geometry(S)
    nsub = C_NSUB
    assert S % tk == 0 and S % tq == 0 and strips.shape == (C_HEADS, tq, width)
    lam_init = 0.8 - 0.6 * math.exp(-0.3 * layer)
    gcb = gate_col // hd
    return pl.pallas_call(
        functools.partial(_diff_kernel, lam_init=lam_init, S=S, tq=tq, tk=tk, nsub=nsub,
                          unroll=math.gcd(S // tk, 4), dmin=dmin, dmax=dmax),
        grid=(C_HEADS, S // tq),
        in_specs=[pl.BlockSpec((None, tq, width), lambda h, i: (h, 0, 0)),
                  pl.BlockSpec((None, 4, C_QK_DIM), lambda h, i: (layer, 0, 0)),
                  pl.BlockSpec((None, 1, C_V_DIM), lambda h, i: (layer, 0, 0)),
                  pl.BlockSpec((None, None, tq, hd), lambda h, i: (h, 0, i, 0)),
                  pl.BlockSpec((None, None, S, hd), lambda h, i: (C_HEADS + h, 0, 0, 0)),
                  pl.BlockSpec((None, None, S, hd), lambda h, i: (2 * C_HEADS + h, 0, 0, 0)),
                  pl.BlockSpec((tq, hd), lambda h, i: (i, gcb + h))],
        out_specs=pl.BlockSpec((tq, hd), lambda h, i: (i, h)),
        out_shape=jax.ShapeDtypeStruct((S, C_HEADS * hd), BF16),
        scratch_shapes=[pltpu.VMEM((S, 2 * C_V_DIM), BF16)],
        compiler_params=_params("arbitrary", "arbitrary"),
        name="diff_attn",
    )(strips, diff_lam, diff_g.reshape(diff_g.shape[0], 1, C_V_DIM), cqkv, cqkv, cqkv, gate_src)


def _short_conv_kernel(b_ref, w_ref, o_ref):
    b = b_ref[...]
    S = b.shape[0]
    row = lax.broadcasted_iota(jnp.int32, b.shape, 0)
    prev = jnp.where(row == 0, 0.0, pltpu.roll(b, 1, 0))
    nxt = jnp.where(row == S - 1, 0.0, pltpu.roll(b, S - 1, 0))
    w = w_ref[...]
    o_ref[...] = w[0:1] * prev + w[1:2] * b + w[2:3] * nxt


def _short_conv(p1, hy_conv, layer, col0):
    S = p1.shape[0]
    nct = BRANCH_W // LANES
    cb = col0 // LANES
    return pl.pallas_call(
        _short_conv_kernel,
        grid=(3 * nct,),
        in_specs=[pl.BlockSpec((S, LANES), lambda c: (0, cb + c)),
                  pl.BlockSpec((None, 3, LANES), lambda c: (layer, 0, c))],
        out_specs=pl.BlockSpec((None, S, LANES), lambda c: (c // nct, 0, c % nct)),
        out_shape=jax.ShapeDtypeStruct((3, S, BRANCH_W), F32),
        compiler_params=_params("parallel"),
        name="short_conv",
    )(p1, hy_conv)


def _filter_dft_kernel(w1t_ref, w1c_ref, w1s_ref, b1_ref, fr_ref, w2_ref, b2_ref, w30f_ref, w30b_ref, w31f_ref,
                       w31b_ref, skip_ref, ad_ref, fa_ref, o_ref, ys_ref, *, L):
    j = pl.program_id(0)
    N = 2 * L
    N1 = fa_ref.shape[1] // 3
    N2 = N // N1
    M = fa_ref.shape[0]
    nch = o_ref.shape[-1] // LANES

    def hidden(pos):
        t = pos * (1.0 / (L - 1))
        w = pos * (2.0 * math.pi / L)
        fb = (1e-4 + lax.broadcasted_iota(jnp.int32, (HY_BANDS, 1), 0).astype(F32)
              * ((HY_BANDS - 1 - 1e-4) / (HY_BANDS - 1)))
        ang = fb * w
        pre = (w1t_ref[...] * t
               + jnp.dot(w1c_ref[...], jnp.cos(ang), precision=HIGHEST, preferred_element_type=F32)
               - jnp.dot(w1s_ref[...], jnp.sin(ang), precision=HIGHEST, preferred_element_type=F32)
               + b1_ref[...])
        fr = fr_ref[...]
        hid = jnp.sin(fr[:, 0:1] * pre)
        hid = jnp.sin(fr[:, 1:2] * (jnp.dot(w2_ref[...], hid, precision=HIGHEST, preferred_element_type=F32)
                                    + b2_ref[...]))
        return hid.T

    def split_rows(x):
        hi, lo = _split_bf16(x)
        return jnp.concatenate([hi, lo, hi], axis=0)

    w3 = [[split_rows(ref[...]) for ref in pair] for pair in ((w30f_ref, w30b_ref), (w31f_ref, w31b_ref))]
    hid0 = _const3(jnp.broadcast_to(hidden(jnp.zeros((1, LANES), F32))[0:1], (SUBLANES, HY_FFN)))
    lag0 = [jnp.dot(hid0, w3[o][1], preferred_element_type=F32)[0:1] + skip_ref[o] for o in range(HY_ORDER)]
    fa3 = fa_ref[...]
    half = N1 // 2
    lag = lambda r: jnp.where(r < L, r, N - r).astype(F32)
    lane = lax.broadcasted_iota(jnp.int32, (1, SUBLANES * N1), 1)
    hid_all = hidden(lag((lane % N1) * N2 + j * SUBLANES + lane // N1))
    rows = [lax.broadcasted_iota(jnp.int32, (N1, 1), 0) * N2 + j * SUBLANES + s for s in range(SUBLANES)]
    hids = [(_const3(hid_all[s * N1:s * N1 + half]), _const3(hid_all[s * N1 + half:(s + 1) * N1]))
            for s in range(SUBLANES)]
    decays = [jnp.exp(-(lag(row) * (1.0 / (L - 1))) * ad_ref[...]) for row in rows]
    pairs = [(s, o) for s in range(SUBLANES) for o in range(HY_ORDER)]
    gs = [jnp.concatenate([jnp.dot(hids[s][0], w3[o][0], preferred_element_type=F32),
                           jnp.dot(hids[s][1], w3[o][1], preferred_element_type=F32)], axis=0) * decays[s]
          for s, o in pairs]
    gs = [jnp.where(rows[0] == L, 0.0, g + jnp.where(rows[0] == 0, lag0[o], 0.0)) if s == 0 else g
          for g, (s, o) in zip(gs, pairs)]
    ys = [_dot3(fa3, g) for g in gs]
    for y, (s, o) in zip(ys, pairs):
        for c in range(nch):
            ys_ref[o, c, pl.ds(s, M, stride=SUBLANES), :] = y[:, c * LANES:(c + 1) * LANES]
    for o in range(HY_ORDER):
        for c in range(nch):
            o_ref[o, :, :, c * LANES:(c + 1) * LANES] = ys_ref[o, c].reshape(M, SUBLANES, LANES)


def _filter_outer_dft(L, fa3, hy_w1, hy_b1, hy_freq, hy_w2, hy_b2, hy_w3, hy_skip, layer):
    assert HY_ORDER == 2
    N = 2 * L
    W = BRANCH_W
    nb = HY_BANDS
    M, N1 = fa3.shape[0], fa3.shape[1] // 3
    N2 = N // N1
    w1 = hy_w1[layer].T
    max_decay = math.log(HY_TARGET) / HY_FAST_PCT
    min_decay = math.log(HY_TARGET) / HY_SLOW_PCT
    absdelta = jnp.abs(jnp.linspace(min_decay, max_decay, W, dtype=F32)).reshape(1, W)
    full = lambda shape: pl.BlockSpec(shape, lambda j: (0,) * len(shape))
    w3_spec = lambda o, bwd: pl.BlockSpec((None, HY_FFN, W), lambda j: (layer, 0, 2 * o + bwd))
    return pl.pallas_call(
        functools.partial(_filter_dft_kernel, L=L),
        grid=(N2 // SUBLANES,),
        in_specs=[full((HY_FFN, 1)), full((HY_FFN, nb)), full((HY_FFN, nb)), full((HY_FFN, 1)), full((HY_FFN, 2)),
                  full((HY_FFN, HY_FFN)), full((HY_FFN, 1)),
                  w3_spec(0, 0), w3_spec(0, 1), w3_spec(1, 0), w3_spec(1, 1),
                  pl.BlockSpec((None, HY_ORDER, 1, W), lambda j: (layer, 0, 0, 0)),
                  full((1, W)), full((M, 3 * N1))],
        out_specs=pl.BlockSpec((HY_ORDER, M, SUBLANES, W), lambda j: (0, 0, j, 0)),
        out_shape=jax.ShapeDtypeStruct((HY_ORDER, M, N2, W), F32),
        scratch_shapes=[pltpu.VMEM((HY_ORDER, W // LANES, M * SUBLANES, LANES), F32)],
        compiler_params=_params("arbitrary"),
        name="filter_dft",
    )(w1[:, 0:1], w1[:, 1:1 + nb], w1[:, 1 + nb:], hy_b1[layer].reshape(HY_FFN, 1), hy_freq[layer].T,
      hy_w2[layer].T, hy_b2[layer].reshape(HY_FFN, 1), hy_w3, hy_w3, hy_w3, hy_w3,
      hy_skip.reshape(hy_skip.shape[0], HY_ORDER, 1, W), absdelta, fa3)


def _outer_dft_kernel(f_ref, x_ref, *rest, gated):
    gate_ref = rest[0] if gated else None
    o_ref, xs_ref, ys_ref = rest[-3:]
    K, sub, tw = x_ref.shape
    M = o_ref.shape[0]
    nch = tw // LANES
    for c in range(nch):
        xs_ref[c] = x_ref[:, :, c * LANES:(c + 1) * LANES].reshape(K * sub, LANES)
    f3 = f_ref[...]
    for s in range(sub):
        x = jnp.concatenate([xs_ref[c, pl.ds(s, K, stride=sub), :] for c in range(nch)], axis=1)
        y = _dot3(f3, x)
        for c in range(nch):
            ys_ref[c, pl.ds(s, M, stride=sub), :] = y[:, c * LANES:(c + 1) * LANES]
    for c in range(nch):
        y = ys_ref[c].reshape(M, sub, LANES)
        if gated:
            y = y * gate_ref[:, :, c * LANES:(c + 1) * LANES]
        o_ref[:, :, c * LANES:(c + 1) * LANES] = y


def _outer_dft(f3, x, xb, gate=None, gb=0, tw=BRANCH_W):
    _, K, N2, W = x.shape
    M = f3.shape[0]
    B = x.shape[0] if xb is None else 1
    x_idx = (lambda b: b) if xb is None else (lambda b: xb)
    blk = lambda rows, idx: pl.BlockSpec((None, rows, SUBLANES, tw), lambda b, t, c: (idx(b), 0, t, c))
    in_specs = [pl.BlockSpec((M, 3 * K), lambda b, t, c: (0, 0)), blk(K, x_idx)]
    args = [f3, x]
    if gate is not None:
        in_specs.append(blk(M, lambda b: gb))
        args.append(gate)
    return pl.pallas_call(
        functools.partial(_outer_dft_kernel, gated=gate is not None),
        grid=(B, N2 // SUBLANES, W // tw),
        in_specs=in_specs,
        out_specs=blk(M, lambda b: b),
        out_shape=jax.ShapeDtypeStruct((B, M, N2, W), F32),
        scratch_shapes=[pltpu.VMEM((tw // LANES, K * SUBLANES, LANES), F32),
                        pltpu.VMEM((tw // LANES, M * SUBLANES, LANES), F32)],
        compiler_params=_params("parallel", "parallel", "parallel"),
        name="dft_outer",
    )(*args)


def _conv_mid_kernel(gf_ref, gi_ref, a_ref, f_ref, o_ref, *, live):
    k = pl.program_id(0)

    @pl.when(k < live)
    def _():
        half = a_ref.shape[0] // 2
        gf3 = gf_ref[...]
        x = _dot3(gf3, a_ref[...])
        h = _dot3(gf3, f_ref[...])
        xr, xi = x[:half], x[half:]
        hr, hi = h[:half], h[half:]
        y = jnp.concatenate([xr * hr - xi * hi, xr * hi + xi * hr], axis=0)
        o_ref[...] = _dot3(gi_ref[...], y)

    @pl.when(k >= live)
    def _():
        o_ref[...] = jnp.zeros(o_ref.shape, F32)


def _conv_mid(gf3, gi3, a, hspec, order, live):
    NS, R, W = a.shape
    src = lambda k: jnp.minimum(k, live - 1)
    return pl.pallas_call(
        functools.partial(_conv_mid_kernel, live=live),
        grid=(NS,),
        in_specs=[pl.BlockSpec((None, R, 3 * R), lambda k: (src(k), 0, 0)),
                  pl.BlockSpec((None, R, 3 * R), lambda k: (src(k), 0, 0)),
                  pl.BlockSpec((None, R, W), lambda k: (src(k), 0, 0)),
                  pl.BlockSpec((None, None, R, W), lambda k: (order, src(k), 0, 0))],
        out_specs=pl.BlockSpec((None, R, W), lambda k: (k, 0, 0)),
        out_shape=jax.ShapeDtypeStruct(a.shape, F32),
        compiler_params=_params("parallel"),
        name="conv_mid",
    )(gf3, gi3, a, hspec)


def _dft_tables(L):
    N = 2 * L
    N2 = LANES
    N1 = N // N2
    NS = pl.cdiv(N1 // 2 + 1, SUBLANES) * SUBLANES
    two_pi = 2.0 * math.pi
    ks = jnp.arange(NS, dtype=jnp.int32)
    t1 = jnp.arange(N1, dtype=jnp.int32)
    live = (ks <= N1 // 2).astype(F32)
    ang = ((ks[:, None] * t1[None, :]) % N1).astype(F32) * (two_pi / N1)
    cs = jnp.stack([jnp.cos(ang), -jnp.sin(ang)], axis=1) * live[:, None, None]
    fa = cs.reshape(2 * NS, N1)
    weight = jnp.where((ks == 0) | (ks == N1 // 2), 1.0, 2.0) / N
    fi = jnp.transpose(cs * weight[:, None, None], (2, 0, 1)).reshape(N1, 2 * NS)[:N1 // 2]
    t2 = jnp.arange(N2, dtype=jnp.int32)
    k = ks[:, None, None] + N1 * t2[None, :, None]
    ph = ((k * t2[None, None, :]) % N).astype(F32) * (two_pi / N)
    gr, gim = jnp.cos(ph), -jnp.sin(ph)
    gf = jnp.concatenate([jnp.concatenate([gr, -gim], axis=2), jnp.concatenate([gim, gr], axis=2)], axis=1)
    gi = jnp.swapaxes(gf, 1, 2)
    return {"fa": _const3(fa), "fh": _const3(fa[:, :N1 // 2]), "fi": _const3(fi), "gf": _const3(gf),
            "gi": _const3(gi), "N1": N1, "N2": N2, "NS": NS}


def _long_conv(z, zi, gate, gi_, hspec, order, tb):
    _, S, W = z.shape
    N1, N2, NS = tb["N1"], tb["N2"], tb["NS"]
    split = lambda t: t.reshape(t.shape[0], N1 // 2, N2, W)
    a = _outer_dft(tb["fh"], split(z), zi)
    b = _conv_mid(tb["gf"], tb["gi"], a.reshape(NS, 2 * N2, W), hspec, order, N1 // 2 + 1)
    y = _outer_dft(tb["fi"], b.reshape(1, 2 * NS, N2, W), 0, split(gate), gi_)
    return y.reshape(1, S, W)


def _merge_kernel(a_ref, zb_ref, bg_ref, c_ref, w_ref, ma_ref, mb_ref, mc_ref, bias_ref, o_ref):
    b = (zb_ref[...] * _silu(bg_ref[...])).astype(BF16)
    bias = bias_ref[...]
    y = None
    for n, (br, m_ref) in enumerate(((a_ref[...], ma_ref), (b, mb_ref), (c_ref[...], mc_ref))):
        pb = jnp.dot(br, w_ref[n], preferred_element_type=F32)
        gate = _sigmoid(m_ref[...] + bias[n:n + 1])
        y = gate * pb if y is None else y + gate * pb
    o_ref[...] = y.astype(o_ref.dtype)


def _merge(a_out, zb, p1, c_out, mg, w_proj_bf, merge_b, layer, tm=256):
    S = a_out.shape[0]
    W, D = BRANCH_W, D_MODEL
    bgb = (P1_COLS - W) // W
    branch = pl.BlockSpec((tm, W), lambda i: (i, 0))
    mspec = lambda n: pl.BlockSpec((tm, D), lambda i: (i, n))
    return pl.pallas_call(
        _merge_kernel,
        grid=(S // tm,),
        in_specs=[branch, pl.BlockSpec((None, tm, W), lambda i: (0, i, 0)),
                  pl.BlockSpec((tm, W), lambda i: (i, bgb)), branch,
                  pl.BlockSpec((None, N_BRANCH, W, D), lambda i: (layer, 0, 0, 0)),
                  mspec(0), mspec(1), mspec(2),
                  pl.BlockSpec((None, N_BRANCH, D), lambda i: (layer, 0, 0))],
        out_specs=pl.BlockSpec((tm, D), lambda i: (i, 0)),
        out_shape=jax.ShapeDtypeStruct((S, D), BF16),
        compiler_params=_params("parallel"),
        name="merge",
    )(a_out, zb, p1, c_out, w_proj_bf, mg, mg, mg, merge_b)


def _out_kernel(x_ref, y_ref, w_ref, g_ref, xo_ref, ho_ref):
    x = x_ref[...] + jnp.dot(y_ref[...], w_ref[...], preferred_element_type=F32)
    xo_ref[...] = x
    ms = jnp.mean(x * x, axis=-1, keepdims=True)
    ho_ref[...] = (x * lax.rsqrt(ms + NORM_EPS) * g_ref[...]).astype(ho_ref.dtype)


def _out_proj(x, y, w_out_bf, layer, g, h_dtype, tm=512):
    S, D = x.shape
    row = pl.BlockSpec((tm, D), lambda i: (i, 0))
    return pl.pallas_call(
        _out_kernel,
        grid=(S // tm,),
        in_specs=[row, row, pl.BlockSpec((None, D, D), lambda i: (layer, 0, 0)), pl.BlockSpec((1, D), lambda i: (0, 0))],
        out_specs=(row, row),
        out_shape=(jax.ShapeDtypeStruct((S, D), F32), jax.ShapeDtypeStruct((S, D), h_dtype)),
        compiler_params=_params("parallel"),
        name="out_proj",
    )(x, y, w_out_bf, g.reshape(1, D))


def _encode(xb, norm_g, final_g, w_in, merge_b, rel_bias, hy_conv, hy_w1, hy_b1, hy_freq, hy_w2, hy_b2, hy_w3,
            hy_skip, diff_lam, diff_g, w_proj_bf, w_out_bf, tb):
    S = xb.shape[0]
    depth = w_in.shape[0]
    N1, N2 = tb["N1"], tb["N2"]
    h = _rmsnorm(xb, norm_g[0], BF16)
    for l in range(depth):
        qkv = [_in_proj(h, w_in, l, COL_A + g * A_GROUP_COLS, A_GROUP_COLS, dil=DIL_PAIRS[g][1]) for g in range(3)]
        p1 = _in_proj(h, w_in, l, COL_P1, P1_COLS)
        cqkv = _in_proj(h, w_in, l, COL_C, C_QKV_COLS, dil=1, scale0=C_Q_SCALE)
        cg = _in_proj(h, w_in, l, COL_P2, BRANCH_W)
        mg = _in_proj(h, w_in, l, COL_P2 + BRANCH_W, N_BRANCH * D_MODEL)

        others = [_dilated_group(rel_bias, qkv[g], g, None, None) for g in (1, 2)]
        a_out = _dilated_group(rel_bias, qkv[0], 0, others, p1, tq=A_CHAINS * A_SUB)

        u = _short_conv(p1, hy_conv, l, BRANCH_W)
        fa_out = _filter_outer_dft(S, tb["fa"], hy_w1, hy_b1, hy_freq, hy_w2, hy_b2, hy_w3, hy_skip, l)
        hspec = fa_out.reshape(HY_ORDER, tb["NS"], 2 * N2, BRANCH_W)
        z = _long_conv(u, 0, u, 1, hspec, 0, tb)
        zb = _long_conv(z, 0, u, 2, hspec, 1, tb)

        c_out = _diff_attention(tb["strips"], cqkv, diff_lam, diff_g, l, cg, 0)

        y = _merge(a_out, zb, p1, c_out, mg, w_proj_bf, merge_b, l)
        if l + 1 < depth:
            xb, h = _out_proj(xb, y, w_out_bf, l, norm_g[l + 1], BF16)
        else:
            _, h = _out_proj(xb, y, w_out_bf, l, final_g, F32)
    return h


def kernel(x, norm_g, final_g, w_in, merge_b, rel_bias, hy_conv, hy_w1, hy_b1, hy_freq, hy_w2, hy_b2, hy_w3,
           hy_skip, diff_lam, diff_g, w_proj, w_out):
    B, S, D = x.shape
    w_proj_bf = w_proj.astype(BF16)
    w_out_bf = w_out.astype(BF16)
    tb = _dft_tables(S)
    tq, _, dmin, _, width = _strip_geometry(S)
    tb["strips"] = _bias_strips(rel_bias, tq, dmin, width)
    outs = [_encode(x[b], norm_g, final_g, w_in, merge_b, rel_bias, hy_conv, hy_w1, hy_b1, hy_freq, hy_w2, hy_b2,
                    hy_w3, hy_skip, diff_lam, diff_g, w_proj_bf, w_out_bf, tb) for b in range(B)]
    return outs[0].reshape(1, S, D) if B == 1 else jnp.stack(outs)
```
